```python
import math
import jax, jax.numpy as jnp
from jax import lax
import numpy as np

D_MODEL = 1024
BATCH = 16
SEQ = 2048
DEPTH = 1

D_MIX = D_MODEL
ATT_HEADS = 8
ATT_KV_HEADS = 2
ATT_HEAD_DIM = 64
ATT_WIDTH = ATT_HEADS * ATT_HEAD_DIM
ATT_KV_WIDTH = ATT_KV_HEADS * ATT_HEAD_DIM
WINDOW = 128
BLOCK = 128
REL_BUCKETS = 32
REL_MAX_DIST = 128
ML_HEADS = 4
ML_HEAD_DIM = 128
ML_WIDTH = ML_HEADS * ML_HEAD_DIM
ML_CHUNK = 128
CONV_WIDTH = 3
N_GATE_COLS = 4 * ML_HEADS
D_FF = 4 * D_MODEL
EPS = 1e-6

SPLITS = list(np.cumsum([ATT_WIDTH, ATT_KV_WIDTH, ATT_KV_WIDTH,
                         ML_WIDTH, ML_WIDTH, ML_WIDTH, ML_WIDTH]))
PROJ_WIDTH = ATT_WIDTH + 2 * ATT_KV_WIDTH + 4 * ML_WIDTH + N_GATE_COLS

kernel_name = 'hymba_swa_mlstm_bidir_block'


def rmsnorm(x, g):
    xf = x.astype(jnp.float32)
    y = xf * lax.rsqrt(jnp.mean(xf * xf, axis=-1, keepdims=True) + EPS)
    return (y * g.astype(jnp.float32)).astype(x.dtype)


def t5_bucket(rel):
    nb = REL_BUCKETS // 2
    max_exact = nb // 2
    ret = jnp.where(rel > 0, nb, 0)
    n = jnp.abs(rel)
    nf = jnp.maximum(n, 1).astype(jnp.float32)
    large = max_exact + (jnp.log(nf / max_exact) / math.log(REL_MAX_DIST / max_exact)
                         * (nb - max_exact)).astype(jnp.int32)
    large = jnp.minimum(large, nb - 1)
    return ret + jnp.where(n < max_exact, n, large)


def windowed_sink_attention(q, k, v, rel_bias, sink):
    B, S = q.shape[0], q.shape[1]
    nb = S // BLOCK
    G = ATT_HEADS // ATT_KV_HEADS
    qb = q.reshape(B, nb, BLOCK, ATT_KV_HEADS, G, ATT_HEAD_DIM)
    pad = ((0, 0), (BLOCK, BLOCK), (0, 0), (0, 0))
    kp = jnp.pad(k, pad).reshape(B, nb + 2, BLOCK, ATT_KV_HEADS, ATT_HEAD_DIM)
    vp = jnp.pad(v, pad).reshape(B, nb + 2, BLOCK, ATT_KV_HEADS, ATT_HEAD_DIM)
    kb = jnp.concatenate([kp[:, :-2], kp[:, 1:-1], kp[:, 2:]], axis=2)
    vb = jnp.concatenate([vp[:, :-2], vp[:, 1:-1], vp[:, 2:]], axis=2)
    s = jnp.einsum('bnqhgd,bnkhd->bnhgqk', qb, kb).astype(jnp.float32) * (ATT_HEAD_DIM ** -0.5)
    r = jnp.arange(BLOCK)[:, None]
    c = jnp.arange(3 * BLOCK)[None, :]
    rel = c - BLOCK - r
    bias = rel_bias.astype(jnp.float32)[t5_bucket(rel)]
    bias = bias.transpose(2, 0, 1).reshape(ATT_KV_HEADS, G, BLOCK, 3 * BLOCK)
    kpos = jnp.arange(nb)[:, None] * BLOCK - BLOCK + c
    valid = (jnp.abs(rel) <= WINDOW)[None] & ((kpos >= 0) & (kpos < S))[:, None, :]
    s = jnp.where(valid[None, :, None, None], s + bias[None, None], -jnp.inf)
    sink_l = sink.astype(jnp.float32).reshape(ATT_KV_HEADS, G)[None, None, :, :, None, None]
    m = jnp.maximum(jnp.max(s, axis=-1, keepdims=True), sink_l)
    p = jnp.exp(s - m)
    p = p / (jnp.sum(p, axis=-1, keepdims=True) + jnp.exp(sink_l - m))
    o = jnp.einsum('bnhgqk,bnkhd->bnqhgd', p.astype(v.dtype), vb)
    return o.reshape(B, S, ATT_WIDTH)


def mlstm_scan(q, k, v, log_i, log_f):
    B, H, S, d = q.shape
    nc = S // ML_CHUNK
    L = ML_CHUNK

    def to_chunks(a):
        return jnp.moveaxis(a.reshape((B, H, nc, L) + a.shape[3:]), 2, 0)

    xs = (to_chunks(q), to_chunks(k), to_chunks(v), to_chunks(log_i), to_chunks(log_f))
    lower = jnp.tril(jnp.ones((L, L), dtype=bool))

    def step(carry, inp):
        C, n, m = carry
        qt, kt, vt, li, lf = inp
        b = jnp.cumsum(lf, axis=-1)
        D = jnp.where(lower, b[..., :, None] - b[..., None, :] + li[..., None, :], -jnp.inf)
        inter = b + m[..., None]
        m_t = jnp.maximum(inter, jnp.max(D, axis=-1))
        w_inter = jnp.exp(inter - m_t)
        qk = jnp.einsum('bhtd,bhsd->bhts', qt, kt) * jnp.exp(D - m_t[..., None])
        num = (jnp.einsum('bhts,bhse->bhte', qk, vt)
               + w_inter[..., None] * jnp.einsum('bhtd,bhde->bhte', qt, C))
        den = jnp.sum(qk, axis=-1) + w_inter * jnp.einsum('bhtd,bhd->bht', qt, n)
        h = num / jnp.maximum(jnp.abs(den), jnp.exp(-m_t))[..., None]
        bL = b[..., -1]
        a = bL[..., None] - b + li
        m_new = jnp.maximum(bL + m, jnp.max(a, axis=-1))
        decay = jnp.exp(bL + m - m_new)
        ws = jnp.exp(a - m_new[..., None])
        C = decay[..., None, None] * C + jnp.einsum('bhs,bhsd,bhse->bhde', ws, kt, vt)
        n = decay[..., None] * n + jnp.einsum('bhs,bhsd->bhd', ws, kt)
        return (C, n, m_new), h

    init = (jnp.zeros((B, H, d, d), jnp.float32), jnp.zeros((B, H, d), jnp.float32),
            jnp.zeros((B, H), jnp.float32))
    _, h = lax.scan(step, init, xs)
    return jnp.moveaxis(h, 0, 2).reshape(B, H, S, d)


def centred_depthwise_conv(x, w):
    half = CONV_WIDTH // 2
    return lax.conv_general_dilated(x, w[:, None, :].astype(x.dtype), window_strides=(1,),
                                    padding=[(half, half)],
                                    dimension_numbers=('NWC', 'WIO', 'NWC'),
                                    feature_group_count=x.shape[-1])


def mlstm_mixer(q_in, k_in, v_in, o_pre, gate_pre, gate_bias, conv_w, norm_g):
    B, S = q_in.shape[0], q_in.shape[1]
    qk = jax.nn.silu(centred_depthwise_conv(jnp.concatenate([q_in, k_in], axis=-1), conv_w))

    def heads(a):
        return a.astype(jnp.float32).reshape(B, S, ML_HEADS, ML_HEAD_DIM).transpose(0, 2, 1, 3)

    q = heads(qk[..., :ML_WIDTH])
    k = heads(qk[..., ML_WIDTH:]) * (ML_HEAD_DIM ** -0.5)
    v = heads(v_in)
    g = (gate_pre.astype(jnp.float32) + gate_bias.astype(jnp.float32))
    g = g.reshape(B, S, 4, ML_HEADS).transpose(2, 0, 3, 1)
    i_f, f_f, i_b, f_b = g[0], g[1], g[2], g[3]
    h_f = mlstm_scan(q, k, v, i_f, jax.nn.log_sigmoid(f_f))
    flip = lambda a: jnp.flip(a, axis=2)
    h_b = flip(mlstm_scan(flip(q), flip(k), flip(v), flip(i_b), flip(jax.nn.log_sigmoid(f_b))))
    h = h_f + h_b
    h = h * lax.rsqrt(jnp.mean(h * h, axis=-1, keepdims=True) + EPS)
    h = h * norm_g.astype(jnp.float32).reshape(ML_HEADS, 1, ML_HEAD_DIM)
    h = h.transpose(0, 2, 1, 3).reshape(B, S, ML_WIDTH)
    return (jax.nn.sigmoid(o_pre.astype(jnp.float32)) * h).astype(v_in.dtype)


def setup_inputs(seed: int = 0) -> dict:
    key = jax.random.key(seed)
    ks = jax.random.split(key, 16)
    f32 = jnp.float32
    nrm = lambda k_, shape: jax.random.normal(k_, shape, f32)
    x = nrm(ks[0], (BATCH, SEQ, D_MODEL))
    norm1_g = 1.0 + 0.02 * nrm(ks[1], (DEPTH, D_MODEL))
    w_in = nrm(ks[2], (DEPTH, D_MODEL, PROJ_WIDTH)) * D_MODEL ** -0.5
    i_bias = 0.1 * nrm(ks[3], (DEPTH, 2, ML_HEADS))
    f_bias = jnp.linspace(3.0, 6.0, ML_HEADS, dtype=f32) + 0.1 * nrm(ks[4], (DEPTH, 2, ML_HEADS))
    b_gates = jnp.stack([i_bias[:, 0], f_bias[:, 0], i_bias[:, 1], f_bias[:, 1]],
                        axis=1).reshape(DEPTH, N_GATE_COLS)
    conv_w = nrm(ks[5], (DEPTH, CONV_WIDTH, 2 * ML_WIDTH)) * CONV_WIDTH ** -0.5
    ml_norm_g = 1.0 + 0.02 * nrm(ks[6], (DEPTH, ML_WIDTH))
    sink_logits = nrm(ks[7], (DEPTH, ATT_HEADS))
    w_out = nrm(ks[8], (DEPTH, D_MIX, D_MODEL)) * D_MIX ** -0.5
    norm2_g = 1.0 + 0.02 * nrm(ks[9], (DEPTH, D_MODEL))
    w_up = nrm(ks[10], (DEPTH, D_MODEL, D_FF)) * D_MODEL ** -0.5
    w_down = nrm(ks[11], (DEPTH, D_FF, D_MODEL)) * D_FF ** -0.5
    rel_bias = 0.5 * nrm(ks[12], (REL_BUCKETS, ATT_HEADS))
    final_g = 1.0 + 0.02 * nrm(ks[13], (D_MODEL,))
    return {'x': x, 'norm1_g': norm1_g, 'w_in': w_in, 'b_gates': b_gates, 'conv_w': conv_w,
            'ml_norm_g': ml_norm_g, 'sink_logits': sink_logits, 'w_out': w_out,
            'norm2_g': norm2_g, 'w_up': w_up, 'w_down': w_down, 'rel_bias': rel_bias,
            'final_g': final_g}


def reference(x, norm1_g, w_in, b_gates, conv_w, ml_norm_g, sink_logits, w_out,
              norm2_g, w_up, w_down, rel_bias, final_g):
    B, S = x.shape[0], x.shape[1]
    for l in range(DEPTH):
        u = rmsnorm(x, norm1_g[l])
        proj = u @ w_in[l]
        q_a, k_a, v_a, q_m, k_m, v_m, o_m, g_m = jnp.split(proj, SPLITS, axis=-1)
        att = windowed_sink_attention(
            q_a.reshape(B, S, ATT_HEADS, ATT_HEAD_DIM),
            k_a.reshape(B, S, ATT_KV_HEADS, ATT_HEAD_DIM),
            v_a.reshape(B, S, ATT_KV_HEADS, ATT_HEAD_DIM),
            rel_bias, sink_logits[l])
        ml = mlstm_mixer(q_m, k_m, v_m, o_m, g_m, b_gates[l], conv_w[l], ml_norm_g[l])
        x = x + jnp.concatenate([att, ml], axis=-1) @ w_out[l]
        hid = rmsnorm(x, norm2_g[l]) @ w_up[l]
        x = x + jnp.square(jax.nn.relu(hid)) @ w_down[l]
    return rmsnorm(x, final_g)
```

```python
import functools
import math

import jax
import jax.numpy as jnp
from jax import lax
from jax.experimental import pallas as pl
from jax.experimental.pallas import tpu as pltpu

D_MODEL = 1024
SEQ = 2048
ATT_HEADS = 8
ATT_KV_HEADS = 2
ATT_HEAD_DIM = 64
ATT_WIDTH = ATT_HEADS * ATT_HEAD_DIM
ATT_KV_WIDTH = ATT_KV_HEADS * ATT_HEAD_DIM
WINDOW = 128
BLOCK = 128
REL_BUCKETS = 32
REL_MAX_DIST = 128
ML_HEADS = 4
ML_HEAD_DIM = 128
ML_WIDTH = ML_HEADS * ML_HEAD_DIM
ML_CHUNK = 128
N_GATE_COLS = 4 * ML_HEADS
D_FF = 4 * D_MODEL
EPS = 1e-6

LANES = 128
N_CHUNKS = SEQ // ML_CHUNK
GATE_ROWS = 8
MAIN_WIDTH = 4 * ML_WIDTH + ATT_WIDTH + 2 * ATT_KV_WIDTH
PROJ_PAD = MAIN_WIDTH + LANES
OFF_QA = 4 * ML_WIDTH
OFF_KA = OFF_QA + ATT_WIDTH
OFF_VA = OFF_KA + ATT_KV_WIDTH

ROW_TILE = 512
FF_CHUNK = 1024
VMEM_LIMIT = 56 * 1024 * 1024

F32 = jnp.float32
BF16 = jnp.bfloat16
NT_DIMS = (((1,), (1,)), ((), ()))


def _dot(a, b):
    return jnp.dot(a, b, preferred_element_type=F32)


def _dot_nt(a, b):
    return lax.dot_general(a, b, NT_DIMS, preferred_element_type=F32)


def _rms(x, g):
    ms = jnp.mean(x * x, axis=-1, keepdims=True)
    return x * lax.rsqrt(ms + EPS) * g


def _inproj_kernel(x_ref, g_ref, w_ref, proj_ref, gt_ref):
    u = _rms(x_ref[...], g_ref[...]).astype(BF16)
    for c0 in range(0, MAIN_WIDTH, 512):
        cw = min(512, MAIN_WIDTH - c0)
        proj_ref[:, c0:c0 + cw] = _dot(u, w_ref[:, c0:c0 + cw]).astype(BF16)
    gates = _dot(u, w_ref[:, MAIN_WIDTH:PROJ_PAD])
    for cc in range(ROW_TILE // ML_CHUNK):
        gt = gates[cc * ML_CHUNK:(cc + 1) * ML_CHUNK, :].T
        for h in range(ML_HEADS):
            gt_ref[h, cc] = gt[h * GATE_ROWS:(h + 1) * GATE_ROWS, :]


def _inproj(x2, g1, w_pad):
    t = x2.shape[0]
    return pl.pallas_call(
        _inproj_kernel,
        grid=(t // ROW_TILE,),
        in_specs=[
            pl.BlockSpec((ROW_TILE, D_MODEL), lambda i: (i, 0)),
            pl.BlockSpec((1, D_MODEL), lambda i: (0, 0)),
            pl.BlockSpec((D_MODEL, PROJ_PAD), lambda i: (0, 0), pipeline_mode=pl.Buffered(1)),
        ],
        out_specs=[
            pl.BlockSpec((ROW_TILE, MAIN_WIDTH), lambda i: (i, 0)),
            pl.BlockSpec((ML_HEADS, ROW_TILE // ML_CHUNK, GATE_ROWS, LANES), lambda i: (0, i, 0, 0)),
        ],
        out_shape=[
            jax.ShapeDtypeStruct((t, MAIN_WIDTH), BF16),
            jax.ShapeDtypeStruct((ML_HEADS, t // ML_CHUNK, GATE_ROWS, LANES), F32),
        ],
        compiler_params=pltpu.CompilerParams(
            dimension_semantics=("arbitrary",), vmem_limit_bytes=VMEM_LIMIT),
        name="inproj",
    )(x2, g1, w_pad)


def _bias_kernel(rb_ref, bucket_ref, out_ref):
    bucket = bucket_ref[...]
    r = lax.broadcasted_iota(jnp.int32, bucket.shape, 0)
    c = lax.broadcasted_iota(jnp.int32, bucket.shape, 1)
    valid = jnp.abs(c - BLOCK - r) <= WINDOW
    for h in range(ATT_HEADS):
        acc = jnp.zeros(bucket.shape, F32)
        for b in range(REL_BUCKETS):
            acc = jnp.where(bucket == b, rb_ref[b, h], acc)
        out_ref[h] = jnp.where(valid, acc, -jnp.inf)


def _bias_table(rel_bias, bucket):
    return pl.pallas_call(
        _bias_kernel,
        in_specs=[
            pl.BlockSpec(memory_space=pltpu.SMEM),
            pl.BlockSpec(memory_space=pltpu.VMEM),
        ],
        out_specs=pl.BlockSpec(memory_space=pltpu.VMEM),
        out_shape=jax.ShapeDtypeStruct((ATT_HEADS, BLOCK, 3 * BLOCK), F32),
        name="bias_table",
    )(rel_bias, bucket)


def _attn_kernel(sink_ref, q_ref, k_ref, v_ref, bias_ref, out_ref,
                 ksw_ref, va_ref, vb_ref, vc_ref, vd_ref):
    half = ATT_HEAD_DIM
    pi = lax.broadcasted_iota(jnp.int32, (LANES, LANES), 0)
    pj = lax.broadcasted_iota(jnp.int32, (LANES, LANES), 1)
    perm = jnp.where((pi + half) % LANES == pj, 1.0, 0.0).astype(BF16)
    ksw_ref[...] = _dot(k_ref[...], perm).astype(BF16)
    v = v_ref[...].astype(F32)
    vsw = _dot(v_ref[...], perm)
    lane_s = lax.broadcasted_iota(jnp.int32, v.shape, 1)
    lo_s = lane_s < half
    va_ref[...] = jnp.where(lo_s, v, 0.0).astype(BF16)
    vb_ref[...] = jnp.where(lo_s, 0.0, vsw).astype(BF16)
    vc_ref[...] = jnp.where(lo_s, vsw, 0.0).astype(BF16)
    vd_ref[...] = jnp.where(lo_s, 0.0, v).astype(BF16)

    lane_q = lax.broadcasted_iota(jnp.int32, (BLOCK, LANES), 1)
    scale = ATT_HEAD_DIM ** -0.5
    m_lo = jnp.where(lane_q < half, scale, 0.0).astype(BF16)
    m_hi = jnp.where(lane_q < half, 0.0, scale).astype(BF16)
    lane_o = lax.broadcasted_iota(jnp.int32, (2 * BLOCK, LANES), 1)

    def block(q0, k0, nk, b0):
        qb = q_ref[pl.ds(q0, BLOCK), :]
        for kv in range(ATT_KV_HEADS):
            base = kv * 2 * LANES
            qp0 = qb[:, base:base + LANES]
            qp1 = qb[:, base + LANES:base + 2 * LANES]
            q_lo = jnp.concatenate([qp0 * m_lo, qp1 * m_lo], axis=0)
            q_hi = jnp.concatenate([qp0 * m_hi, qp1 * m_hi], axis=0)
            k_lo = (k_ref if kv == 0 else ksw_ref)[pl.ds(k0, nk), :]
            k_hi = (ksw_ref if kv == 0 else k_ref)[pl.ds(k0, nk), :]
            s_lo = _dot_nt(q_lo, k_lo)
            s_hi = _dot_nt(q_hi, k_hi)
            probs, denoms = [], []
            for s_pair, h_off in ((s_lo, 0), (s_hi, 1)):
                p_rows, l_rows = [], []
                for part in range(2):
                    h = kv * 4 + 2 * part + h_off
                    s = s_pair[part * BLOCK:(part + 1) * BLOCK, :] + bias_ref[h, :, b0:b0 + nk]
                    sink = sink_ref[h]
                    m = jnp.maximum(jnp.max(s, axis=-1, keepdims=True), sink)
                    p = jnp.exp(s - m)
                    l_rows.append(jnp.sum(p, axis=-1, keepdims=True) + jnp.exp(sink - m))
                    p_rows.append(p.astype(BF16))
                probs.append(jnp.concatenate(p_rows, axis=0))
                denoms.append(jnp.concatenate(l_rows, axis=0))
            v_lo = (va_ref if kv == 0 else vc_ref)[pl.ds(k0, nk), :]
            v_hi = (vb_ref if kv == 0 else vd_ref)[pl.ds(k0, nk), :]
            o = _dot(probs[0], v_lo) + _dot(probs[1], v_hi)
            o = o * jnp.where(lane_o < half, 1.0 / denoms[0], 1.0 / denoms[1])
            out_ref[pl.ds(q0, BLOCK), base:base + LANES] = o[:BLOCK].astype(BF16)
            out_ref[pl.ds(q0, BLOCK), base + LANES:base + 2 * LANES] = o[BLOCK:].astype(BF16)

    nb = SEQ // BLOCK
    block(0, 0, 2 * BLOCK, BLOCK)

    def mid(j, carry):
        q0 = pl.multiple_of(j * BLOCK, BLOCK)
        block(q0, pl.multiple_of(q0 - BLOCK, BLOCK), 3 * BLOCK, 0)
        return carry

    lax.fori_loop(1, nb - 1, mid, 0)
    block((nb - 1) * BLOCK, (nb - 2) * BLOCK, 2 * BLOCK, 0)


def _attention(proj, bias, sink, batch):
    t = proj.shape[0]
    kv_spec = lambda off: pl.BlockSpec((SEQ, ATT_KV_WIDTH), lambda b: (b, off // ATT_KV_WIDTH))
    return pl.pallas_call(
        _attn_kernel,
        grid=(batch,),
        in_specs=[
            pl.BlockSpec(memory_space=pltpu.SMEM),
            pl.BlockSpec((SEQ, ATT_WIDTH), lambda b: (b, OFF_QA // ATT_WIDTH)),
            kv_spec(OFF_KA),
            kv_spec(OFF_VA),
            pl.BlockSpec((ATT_HEADS, BLOCK, 3 * BLOCK), lambda b: (0, 0, 0)),
        ],
        out_specs=pl.BlockSpec((SEQ, ATT_WIDTH), lambda b: (b, 0)),
        out_shape=jax.ShapeDtypeStruct((t, ATT_WIDTH), BF16),
        scratch_shapes=[pltpu.VMEM((SEQ, ATT_KV_WIDTH), BF16) for _ in range(5)],
        compiler_params=pltpu.CompilerParams(
            dimension_semantics=("arbitrary",), vmem_limit_bytes=VMEM_LIMIT),
        name="attention",
    )(sink, proj, proj, proj, bias)


def _log_sigmoid(x):
    return jnp.minimum(x, 0.0) - jnp.log1p(jnp.exp(-jnp.abs(x)))


def _split3(x):
    hi = x.astype(BF16)
    r1 = x - hi.astype(F32)
    mid = r1.astype(BF16)
    lo = (r1 - mid.astype(F32)).astype(BF16)
    return hi, mid, lo


def _mlstm_kernel(q_ref, k_ref, v_ref, o_ref, gt_ref, gb_ref, cwq_ref, cwk_ref, ng_ref, out_ref,
                  qs_ref, kt_ref, hf_ref, hb_ref, cf_ref, cb_ref):
    L = ML_CHUNK
    d = ML_HEAD_DIM
    nh = ML_HEADS
    ri = lax.broadcasted_iota(jnp.int32, (L, L), 0)
    ci = lax.broadcasted_iota(jnp.int32, (L, L), 1)
    tri_lo = ri >= ci
    tri_up = ri <= ci
    ones_le = jnp.where(ri <= ci, 1.0, 0.0).astype(BF16)
    ones_ge = jnp.where(ri >= ci, 1.0, 0.0).astype(BF16)
    e0 = jnp.where(ci == 0, 1.0, 0.0).astype(BF16)
    k_scale = ML_HEAD_DIM ** -0.5

    def prep(c, carry):
        r0 = pl.multiple_of(c * L, L)
        rp = pl.multiple_of(jnp.maximum(r0 - 16, 0), 16)
        rn = pl.multiple_of(jnp.minimum(r0 + L, SEQ - 16), 16)
        first = c == 0
        last = c == N_CHUNKS - 1
        for hh in range(nh):
            sl = slice(hh * d, (hh + 1) * d)
            for src, cw_ref, is_k in ((q_ref, cwq_ref, False), (k_ref, cwk_ref, True)):
                cur = src[pl.ds(r0, L), sl].astype(F32)
                halo_p = src[pl.ds(rp, 16), sl].astype(F32)[15:16, :]
                halo_n = src[pl.ds(rn, 16), sl].astype(F32)[0:1, :]
                halo_p = jnp.where(first, 0.0, halo_p)
                halo_n = jnp.where(last, 0.0, halo_n)
                prev = jnp.where(ri == 0, halo_p, pltpu.roll(cur, 1, 0))
                nxt = jnp.where(ri == L - 1, halo_n, pltpu.roll(cur, L - 1, 0))
                w = cw_ref[:, sl]
                y = w[0:1, :] * prev + w[1:2, :] * cur + w[2:3, :] * nxt
                y = y * (1.0 / (1.0 + jnp.exp(-y)))
                if is_k:
                    kt_ref[hh, c] = (y * k_scale).T.astype(BF16)
                else:
                    qs_ref[hh, pl.ds(r0, L), :] = y.astype(BF16)
        return carry

    lax.fori_loop(0, N_CHUNKS, prep, 0)

    cf_ref[...] = jnp.zeros(cf_ref.shape, F32)
    cb_ref[...] = jnp.zeros(cb_ref.shape, F32)

    def cumsum_rows(x, ones):
        return sum(_dot(p, ones) for p in _split3(x))

    def cumsum_cols(ones, x):
        return sum(_dot_nt(ones, p) for p in _split3(x))

    def step(hh, c, tri, b_col, b_row, li_row, b_last, m, c_ref, h_ref):
        sl = slice(hh * d, (hh + 1) * d)
        r0 = pl.multiple_of(c * L, L)
        qc = qs_ref[hh, pl.ds(r0, L), :]
        ktc = kt_ref[hh, c]
        vaug = jnp.concatenate([v_ref[pl.ds(r0, L), sl], e0], axis=1)
        qk = _dot(qc, ktc)
        dmat = jnp.where(tri, b_col + (li_row - b_row), -jnp.inf)
        inter = b_col + m
        m_t = jnp.maximum(inter, jnp.max(dmat, axis=-1, keepdims=True))
        w_inter = jnp.exp(inter - m_t)
        sm = qk * jnp.exp(dmat - m_t)
        lhs = jnp.concatenate([sm.astype(BF16), (qc.astype(F32) * w_inter).astype(BF16)], axis=1)
        state = c_ref[hh]
        rhs = jnp.concatenate([vaug, state.astype(BF16)], axis=0)
        nd = _dot(lhs, rhs)
        den = nd[:, d:d + 1]
        h_ref[hh, pl.ds(r0, L), :] = nd[:, :d] / jnp.maximum(jnp.abs(den), jnp.exp(-m_t))
        a_row = b_last - b_row + li_row
        m_new = jnp.maximum(b_last + m, jnp.max(a_row, axis=-1, keepdims=True))
        decay = jnp.exp(b_last + m - m_new)
        ws = jnp.exp(a_row - m_new)
        kw = (ktc.astype(F32) * ws).astype(BF16)
        c_ref[hh] = decay * state + _dot(kw, vaug)
        return m_new

    def body(i, ms):
        cf = i
        cb = N_CHUNKS - 1 - i
        g_f = jnp.concatenate([gt_ref[hh, cf] + gb_ref[hh] for hh in range(nh)], axis=0)
        g_b = jnp.concatenate([gt_ref[hh, cb] + gb_ref[hh] for hh in range(nh)], axis=0)
        lf_f = _log_sigmoid(g_f)
        lf_b = _log_sigmoid(g_b)
        bf_rows = cumsum_rows(lf_f, ones_le)
        bb_rows = cumsum_rows(lf_b, ones_ge)
        bf_cols = cumsum_cols(ones_ge, lf_f)
        bb_cols = cumsum_cols(ones_le, lf_b)
        out = []
        for hh in range(nh):
            g0 = hh * GATE_ROWS
            b_row = bf_rows[g0 + 1:g0 + 2, :]
            out.append(step(hh, cf, tri_lo, bf_cols[:, g0 + 1:g0 + 2], b_row, g_f[g0:g0 + 1, :],
                            b_row[:, L - 1:L], ms[2 * hh], cf_ref, hf_ref))
            rb_row = bb_rows[g0 + 3:g0 + 4, :]
            out.append(step(hh, cb, tri_up, bb_cols[:, g0 + 3:g0 + 4], rb_row, g_b[g0 + 2:g0 + 3, :],
                            rb_row[:, 0:1], ms[2 * hh + 1], cb_ref, hb_ref))
        return tuple(out)

    lax.fori_loop(0, N_CHUNKS, body, tuple(jnp.zeros((1, 1), F32) for _ in range(2 * nh)))

    for hh in range(nh):
        sl = slice(hh * d, (hh + 1) * d)
        h = _rms(hf_ref[hh] + hb_ref[hh], ng_ref[:, sl])
        o = o_ref[:, sl].astype(F32)
        out_ref[:, sl] = (h * (1.0 / (1.0 + jnp.exp(-o)))).astype(BF16)


def _mlstm(proj, gt, gbias, conv_w, norm_g, batch):
    t = proj.shape[0]
    col = lambda off: pl.BlockSpec((SEQ, ML_WIDTH), lambda b: (b, off // ML_WIDTH))
    return pl.pallas_call(
        _mlstm_kernel,
        grid=(batch,),
        in_specs=[
            col(0), col(ML_WIDTH), col(2 * ML_WIDTH), col(3 * ML_WIDTH),
            pl.BlockSpec((ML_HEADS, N_CHUNKS, GATE_ROWS, LANES), lambda b: (0, b, 0, 0)),
            pl.BlockSpec((ML_HEADS, GATE_ROWS, 1), lambda b: (0, 0, 0)),
            pl.BlockSpec((3, ML_WIDTH), lambda b: (0, 0)),
            pl.BlockSpec((3, ML_WIDTH), lambda b: (0, 1)),
            pl.BlockSpec((1, ML_WIDTH), lambda b: (0, 0)),
        ],
        out_specs=pl.BlockSpec((SEQ, ML_WIDTH), lambda b: (b, 0)),
        out_shape=jax.ShapeDtypeStruct((t, ML_WIDTH), BF16),
        scratch_shapes=[
            pltpu.VMEM((ML_HEADS, SEQ, ML_HEAD_DIM), BF16),
            pltpu.VMEM((ML_HEADS, N_CHUNKS, ML_HEAD_DIM, ML_CHUNK), BF16),
            pltpu.VMEM((ML_HEADS, SEQ, ML_HEAD_DIM), F32),
            pltpu.VMEM((ML_HEADS, SEQ, ML_HEAD_DIM), F32),
            pltpu.VMEM((ML_HEADS, ML_HEAD_DIM, 2 * ML_HEAD_DIM), F32),
            pltpu.VMEM((ML_HEADS, ML_HEAD_DIM, 2 * ML_HEAD_DIM), F32),
        ],
        compiler_params=pltpu.CompilerParams(
            dimension_semantics=("arbitrary",), vmem_limit_bytes=VMEM_LIMIT),
        name="mlstm",
    )(proj, proj, proj, proj, gt, gbias, conv_w, conv_w, norm_g)


def _ffn_kernel(att_ref, ml_ref, x_ref, wo_ref, g2_ref, wup_ref, wdn_ref, gf_ref, out_ref):
    x1 = (x_ref[...] + _dot(att_ref[...], wo_ref[:ATT_WIDTH, :])
          + _dot(ml_ref[...], wo_ref[ATT_WIDTH:, :]))
    hn = _rms(x1, g2_ref[...]).astype(BF16)
    out_ref[...] = x1
    for c0 in range(0, D_FF, FF_CHUNK):
        hid = _dot(hn, wup_ref[:, c0:c0 + FF_CHUNK])
        act = jnp.square(jnp.maximum(hid, 0.0)).astype(BF16)
        out_ref[...] += _dot(act, wdn_ref[c0:c0 + FF_CHUNK, :])
    out_ref[...] = _rms(out_ref[...], gf_ref[...])


def _ffn(att, ml, x2, w_out, g2, w_up, w_down, gf):
    t = x2.shape[0]
    row = lambda w: pl.BlockSpec((ROW_TILE, w), lambda i: (i, 0))
    whole = lambda a: pl.BlockSpec(a.shape, lambda i: (0, 0), pipeline_mode=pl.Buffered(1))
    return pl.pallas_call(
        _ffn_kernel,
        grid=(t // ROW_TILE,),
        in_specs=[row(ATT_WIDTH), row(ML_WIDTH), row(D_MODEL), whole(w_out), whole(g2),
                  whole(w_up), whole(w_down), whole(gf)],
        out_specs=row(D_MODEL),
        out_shape=jax.ShapeDtypeStruct((t, D_MODEL), F32),
        compiler_params=pltpu.CompilerParams(
            dimension_semantics=("arbitrary",), vmem_limit_bytes=VMEM_LIMIT),
        name="outproj_ffn",
    )(att, ml, x2, w_out, g2, w_up, w_down, gf)


def _t5_bucket(rel):
    nb = REL_BUCKETS // 2
    max_exact = nb // 2
    ret = jnp.where(rel > 0, nb, 0)
    n = jnp.abs(rel)
    nf = jnp.maximum(n, 1).astype(jnp.float32)
    large = max_exact + (jnp.log(nf / max_exact) / math.log(REL_MAX_DIST / max_exact)
                         * (nb - max_exact)).astype(jnp.int32)
    large = jnp.minimum(large, nb - 1)
    return ret + jnp.where(n < max_exact, n, large)


def kernel(x, norm1_g, w_in, b_gates, conv_w, ml_norm_g, sink_logits, w_out, norm2_g, w_up, w_down,
           rel_bias, final_g):
    batch, seq, d_model = x.shape
    assert (seq, d_model) == (SEQ, D_MODEL) and w_in.shape[0] == 1
    x2 = x.reshape(batch * seq, d_model)

    w = w_in[0]
    att_cols = ATT_WIDTH + 2 * ATT_KV_WIDTH
    gate_cols = w[:, att_cols + 4 * ML_WIDTH:].reshape(d_model, 4, ML_HEADS).transpose(0, 2, 1)
    gate_cols = jnp.pad(gate_cols, ((0, 0), (0, 0), (0, GATE_ROWS - 4))).reshape(d_model, -1)
    gate_cols = jnp.pad(gate_cols, ((0, 0), (0, LANES - gate_cols.shape[1])))
    w_pad = jnp.concatenate([w[:, att_cols:att_cols + 4 * ML_WIDTH], w[:, :att_cols], gate_cols],
                            axis=1).astype(BF16)
    gbias = jnp.pad(b_gates[0].reshape(4, ML_HEADS).T, ((0, 0), (0, GATE_ROWS - 4)))[..., None]

    proj, gt = _inproj(x2, norm1_g, w_pad)

    r = jnp.arange(BLOCK)[:, None]
    c = jnp.arange(3 * BLOCK)[None, :]
    bucket = _t5_bucket(c - BLOCK - r).astype(jnp.int32)
    bias = _bias_table(rel_bias.astype(F32), bucket)
    att = _attention(proj, bias, sink_logits[0].astype(F32), batch)

    ml = _mlstm(proj, gt, gbias.astype(F32), conv_w[0], ml_norm_g, batch)

    out = _ffn(att, ml, x2, w_out[0].astype(BF16), norm2_g, w_up[0].astype(BF16),
               w_down[0].astype(BF16), final_g.reshape(1, d_model))
    return out.reshape(batch, seq, d_model)
```

```python
import math

import jax
import jax.numpy as jnp
from jax import lax
from jax.experimental import pallas as pl
from jax.experimental.pallas import tpu as pltpu

D_MODEL = 1024
SEQ = 2048
ATT_HEADS = 8
ATT_KV_HEADS = 2
ATT_HEAD_DIM = 64
ATT_WIDTH = ATT_HEADS * ATT_HEAD_DIM
ATT_KV_WIDTH = ATT_KV_HEADS * ATT_HEAD_DIM
WINDOW = 128
BLOCK = 128
REL_BUCKETS = 32
REL_MAX_DIST = 128
ML_HEADS = 4
ML_HEAD_DIM = 128
ML_WIDTH = ML_HEADS * ML_HEAD_DIM
ML_CHUNK = 128
N_GATE_COLS = 4 * ML_HEADS
D_FF = 4 * D_MODEL
EPS = 1e-6

LANES = 128
N_CHUNKS = SEQ // ML_CHUNK
GATE_ROWS = 8
MAIN_WIDTH = 4 * ML_WIDTH + ATT_WIDTH + 2 * ATT_KV_WIDTH
PROJ_PAD = MAIN_WIDTH + LANES
OFF_QA = 4 * ML_WIDTH
OFF_KA = OFF_QA + ATT_WIDTH
OFF_VA = OFF_KA + ATT_KV_WIDTH

ROW_TILE = 512
FF_CHUNK = 1024
VMEM_LIMIT = 56 * 1024 * 1024

F32 = jnp.float32
BF16 = jnp.bfloat16
NT_DIMS = (((1,), (1,)), ((), ()))


def _dot(a, b):
    return jnp.dot(a, b, preferred_element_type=F32)


def _dot_nt(a, b):
    return lax.dot_general(a, b, NT_DIMS, preferred_element_type=F32)


def _rms(x, g):
    ms = jnp.mean(x * x, axis=-1, keepdims=True)
    return x * lax.rsqrt(ms + EPS) * g


def _inproj_kernel(x_ref, g_ref, w_ref, proj_ref, gt_ref):
    u = _rms(x_ref[...], g_ref[...]).astype(BF16)
    for c0 in range(0, MAIN_WIDTH, 512):
        cw = min(512, MAIN_WIDTH - c0)
        proj_ref[:, c0:c0 + cw] = _dot(u, w_ref[:, c0:c0 + cw]).astype(BF16)
    gates = _dot(u, w_ref[:, MAIN_WIDTH:PROJ_PAD])
    for cc in range(ROW_TILE // ML_CHUNK):
        gt = gates[cc * ML_CHUNK:(cc + 1) * ML_CHUNK, :].T
        for h in range(ML_HEADS):
            gt_ref[h, cc] = gt[h * GATE_ROWS:(h + 1) * GATE_ROWS, :]


def _inproj(x2, g1, w_pad):
    t = x2.shape[0]
    return pl.pallas_call(
        _inproj_kernel,
        grid=(t // ROW_TILE,),
        in_specs=[
            pl.BlockSpec((ROW_TILE, D_MODEL), lambda i: (i, 0)),
            pl.BlockSpec((1, D_MODEL), lambda i: (0, 0)),
            pl.BlockSpec((D_MODEL, PROJ_PAD), lambda i: (0, 0), pipeline_mode=pl.Buffered(1)),
        ],
        out_specs=[
            pl.BlockSpec((ROW_TILE, MAIN_WIDTH), lambda i: (i, 0)),
            pl.BlockSpec((ML_HEADS, ROW_TILE // ML_CHUNK, GATE_ROWS, LANES), lambda i: (0, i, 0, 0)),
        ],
        out_shape=[
            jax.ShapeDtypeStruct((t, MAIN_WIDTH), BF16),
            jax.ShapeDtypeStruct((ML_HEADS, t // ML_CHUNK, GATE_ROWS, LANES), F32),
        ],
        compiler_params=pltpu.CompilerParams(
            dimension_semantics=("arbitrary",), vmem_limit_bytes=VMEM_LIMIT),
        name="inproj",
    )(x2, g1, w_pad)


def _bias_kernel(rb_ref, bucket_ref, out_ref):
    bucket = bucket_ref[...]
    r = lax.broadcasted_iota(jnp.int32, bucket.shape, 0)
    c = lax.broadcasted_iota(jnp.int32, bucket.shape, 1)
    valid = jnp.abs(c - BLOCK - r) <= WINDOW
    for h in range(ATT_HEADS):
        acc = jnp.zeros(bucket.shape, F32)
        for b in range(REL_BUCKETS):
            acc = jnp.where(bucket == b, rb_ref[b, h], acc)
        out_ref[h] = jnp.where(valid, acc, -jnp.inf)


def _bias_table(rel_bias, bucket):
    return pl.pallas_call(
        _bias_kernel,
        in_specs=[
            pl.BlockSpec(memory_space=pltpu.SMEM),
            pl.BlockSpec(memory_space=pltpu.VMEM),
        ],
        out_specs=pl.BlockSpec(memory_space=pltpu.VMEM),
        out_shape=jax.ShapeDtypeStruct((ATT_HEADS, BLOCK, 3 * BLOCK), F32),
        name="bias_table",
    )(rel_bias, bucket)


def _attn_kernel(sink_ref, q_ref, k_ref, v_ref, bias_ref, out_ref,
                 ksw_ref, va_ref, vb_ref, vc_ref, vd_ref):
    half = ATT_HEAD_DIM
    pi = lax.broadcasted_iota(jnp.int32, (LANES, LANES), 0)
    pj = lax.broadcasted_iota(jnp.int32, (LANES, LANES), 1)
    perm = jnp.where((pi + half) % LANES == pj, 1.0, 0.0).astype(BF16)
    ksw_ref[...] = _dot(k_ref[...], perm).astype(BF16)
    v = v_ref[...].astype(F32)
    vsw = _dot(v_ref[...], perm)
    lane_s = lax.broadcasted_iota(jnp.int32, v.shape, 1)
    lo_s = lane_s < half
    va_ref[...] = jnp.where(lo_s, v, 0.0).astype(BF16)
    vb_ref[...] = jnp.where(lo_s, 0.0, vsw).astype(BF16)
    vc_ref[...] = jnp.where(lo_s, vsw, 0.0).astype(BF16)
    vd_ref[...] = jnp.where(lo_s, 0.0, v).astype(BF16)

    lane_q = lax.broadcasted_iota(jnp.int32, (BLOCK, LANES), 1)
    scale = ATT_HEAD_DIM ** -0.5
    m_lo = jnp.where(lane_q < half, scale, 0.0).astype(BF16)
    m_hi = jnp.where(lane_q < half, 0.0, scale).astype(BF16)
    lane_o = lax.broadcasted_iota(jnp.int32, (2 * BLOCK, LANES), 1)

    def block(q0, k0, nk, b0):
        qb = q_ref[pl.ds(q0, BLOCK), :]
        for kv in range(ATT_KV_HEADS):
            base = kv * 2 * LANES
            qp0 = qb[:, base:base + LANES]
            qp1 = qb[:, base + LANES:base + 2 * LANES]
            q_lo = jnp.concatenate([qp0 * m_lo, qp1 * m_lo], axis=0)
            q_hi = jnp.concatenate([qp0 * m_hi, qp1 * m_hi], axis=0)
            k_lo = (k_ref if kv == 0 else ksw_ref)[pl.ds(k0, nk), :]
            k_hi = (ksw_ref if kv == 0 else k_ref)[pl.ds(k0, nk), :]
            s_lo = _dot_nt(q_lo, k_lo)
            s_hi = _dot_nt(q_hi, k_hi)
            probs, denoms = [], []
            for s_pair, h_off in ((s_lo, 0), (s_hi, 1)):
                p_rows, l_rows = [], []
                for part in range(2):
                    h = kv * 4 + 2 * part + h_off
                    s = s_pair[part * BLOCK:(part + 1) * BLOCK, :] + bias_ref[h, :, b0:b0 + nk]
                    sink = sink_ref[h]
                    m = jnp.maximum(jnp.max(s, axis=-1, keepdims=True), sink)
                    p = jnp.exp(s - m)
                    l_rows.append(jnp.sum(p, axis=-1, keepdims=True) + jnp.exp(sink - m))
                    p_rows.append(p.astype(BF16))
                probs.append(jnp.concatenate(p_rows, axis=0))
                denoms.append(jnp.concatenate(l_rows, axis=0))
            v_lo = (va_ref if kv == 0 else vc_ref)[pl.ds(k0, nk), :]
            v_hi = (vb_ref if kv == 0 else vd_ref)[pl.ds(k0, nk), :]
            o = _dot(probs[0], v_lo) + _dot(probs[1], v_hi)
            o = o * jnp.where(lane_o < half, 1.0 / denoms[0], 1.0 / denoms[1])
            out_ref[pl.ds(q0, BLOCK), base:base + LANES] = o[:BLOCK].astype(BF16)
            out_ref[pl.ds(q0, BLOCK), base + LANES:base + 2 * LANES] = o[BLOCK:].astype(BF16)

    nb = SEQ // BLOCK
    block(0, 0, 2 * BLOCK, BLOCK)

    def mid(j, carry):
        q0 = pl.multiple_of(j * BLOCK, BLOCK)
        block(q0, pl.multiple_of(q0 - BLOCK, BLOCK), 3 * BLOCK, 0)
        return carry

    lax.fori_loop(1, nb - 1, mid, 0)
    block((nb - 1) * BLOCK, (nb - 2) * BLOCK, 2 * BLOCK, 0)


def _attention(proj, bias, sink, batch):
    t = proj.shape[0]
    kv_spec = lambda off: pl.BlockSpec((SEQ, ATT_KV_WIDTH), lambda b: (b, off // ATT_KV_WIDTH))
    return pl.pallas_call(
        _attn_kernel,
        grid=(batch,),
        in_specs=[
            pl.BlockSpec(memory_space=pltpu.SMEM),
            pl.BlockSpec((SEQ, ATT_WIDTH), lambda b: (b, OFF_QA // ATT_WIDTH)),
            kv_spec(OFF_KA),
            kv_spec(OFF_VA),
            pl.BlockSpec((ATT_HEADS, BLOCK, 3 * BLOCK), lambda b: (0, 0, 0)),
        ],
        out_specs=pl.BlockSpec((SEQ, ATT_WIDTH), lambda b: (b, 0)),
        out_shape=jax.ShapeDtypeStruct((t, ATT_WIDTH), BF16),
        scratch_shapes=[pltpu.VMEM((SEQ, ATT_KV_WIDTH), BF16) for _ in range(5)],
        compiler_params=pltpu.CompilerParams(
            dimension_semantics=("arbitrary",), vmem_limit_bytes=VMEM_LIMIT),
        name="attention",
    )(sink, proj, proj, proj, bias)


def _log_sigmoid(x):
    return jnp.minimum(x, 0.0) - jnp.log1p(jnp.exp(-jnp.abs(x)))


def _split3(x):
    hi = x.astype(BF16)
    r1 = x - hi.astype(F32)
    mid = r1.astype(BF16)
    lo = (r1 - mid.astype(F32)).astype(BF16)
    return hi, mid, lo


AUG_ROWS = ML_HEAD_DIM + 16
ROWS_PER_CHAIN = 6
N_CHAIN_ROWS = 2 * ML_HEADS * ROWS_PER_CHAIN


def _mlstm_kernel(q_ref, k_ref, v_ref, o_ref, gt_ref, gb_ref, cwq_ref, cwk_ref, ng_ref, out_ref,
                  qt_ref, ks_ref, vt_ref, hf_ref, hb_ref, cf_ref, cb_ref, rows_ref):
    L = ML_CHUNK
    d = ML_HEAD_DIM
    nh = ML_HEADS
    si = lax.broadcasted_iota(jnp.int32, (L, L), 0)
    ti = lax.broadcasted_iota(jnp.int32, (L, L), 1)
    masks = (si <= ti, si >= ti)
    ones_le = jnp.where(si <= ti, 1.0, 0.0).astype(BF16)
    ones_ge = jnp.where(si >= ti, 1.0, 0.0).astype(BF16)
    eye = jnp.where(si == ti, 1.0, 0.0).astype(BF16)
    aug_tail = jnp.where(lax.broadcasted_iota(jnp.int32, (AUG_ROWS - d, L), 0) == 0, 1.0, 0.0).astype(BF16)
    k_scale = ML_HEAD_DIM ** -0.5

    def prep(c, carry):
        r0 = pl.multiple_of(c * L, L)
        rp = pl.multiple_of(jnp.maximum(r0 - 16, 0), 16)
        rn = pl.multiple_of(jnp.minimum(r0 + L, SEQ - 16), 16)
        first = c == 0
        last = c == N_CHUNKS - 1
        for hh in range(nh):
            sl = slice(hh * d, (hh + 1) * d)
            for src, cw_ref, is_k in ((q_ref, cwq_ref, False), (k_ref, cwk_ref, True)):
                cur = src[pl.ds(r0, L), sl].astype(F32)
                halo_p = src[pl.ds(rp, 16), sl].astype(F32)[15:16, :]
                halo_n = src[pl.ds(rn, 16), sl].astype(F32)[0:1, :]
                halo_p = jnp.where(first, 0.0, halo_p)
                halo_n = jnp.where(last, 0.0, halo_n)
                prev = jnp.where(si == 0, halo_p, pltpu.roll(cur, 1, 0))
                nxt = jnp.where(si == L - 1, halo_n, pltpu.roll(cur, L - 1, 0))
                w = cw_ref[:, sl]
                y = w[0:1, :] * prev + w[1:2, :] * cur + w[2:3, :] * nxt
                y = y * (1.0 / (1.0 + jnp.exp(-y)))
                if is_k:
                    ks_ref[hh, pl.ds(r0, L), :] = (y * k_scale).astype(BF16)
                else:
                    qt_ref[hh, c] = y.T.astype(BF16)
            vt = v_ref[pl.ds(r0, L), sl].astype(F32).T.astype(BF16)
            vt_ref[hh, c] = jnp.concatenate([vt, aug_tail], axis=0)
        return carry

    lax.fori_loop(0, N_CHUNKS, prep, 0)

    lane16 = lax.broadcasted_iota(jnp.int32, (N_CHUNKS, L), 1)
    row16 = lax.broadcasted_iota(jnp.int32, (N_CHUNKS, L), 0)
    for hh in range(nh):
        for direction in range(2):
            fwd = direction == 0
            k_i, k_f = 2 * direction, 2 * direction + 1
            li = gt_ref[hh, :, k_i, :] + gb_ref[hh, k_i:k_i + 1, :]
            lf = _log_sigmoid(gt_ref[hh, :, k_f, :] + gb_ref[hh, k_f:k_f + 1, :])
            ones = ones_le if fwd else ones_ge
            b = sum(_dot(p, ones) for p in _split3(lf))
            b_last = jnp.broadcast_to(b[:, L - 1:L] if fwd else b[:, 0:1], (N_CHUNKS, L))
            a = b_last - b + li
            a_max = jnp.broadcast_to(jnp.max(a, axis=-1, keepdims=True), (N_CHUNKS, L))
            r = li - b
            cm = r
            for sh in (1, 2, 4, 8, 16, 32, 64):
                if fwd:
                    cm = jnp.maximum(cm, jnp.where(lane16 >= sh, pltpu.roll(cm, sh, 1), -jnp.inf))
                else:
                    cm = jnp.maximum(cm, jnp.where(lane16 < L - sh, pltpu.roll(cm, L - sh, 1), -jnp.inf))
            m_cur = jnp.zeros((1, L), F32)
            m_prev = jnp.zeros((N_CHUNKS, L), F32)
            m_new = jnp.zeros((N_CHUNKS, L), F32)
            for c in (range(N_CHUNKS) if fwd else reversed(range(N_CHUNKS))):
                m_prev = jnp.where(row16 == c, m_cur, m_prev)
                m_cur = jnp.maximum(b_last[c:c + 1, :] + m_cur, a_max[c:c + 1, :])
                m_new = jnp.where(row16 == c, m_cur, m_new)
            m_t = b + jnp.maximum(m_prev, cm)
            base = (2 * hh + direction) * ROWS_PER_CHAIN
            for j, val in enumerate((b - m_t, jnp.exp(b + m_prev - m_t), jnp.exp(-m_t),
                                     jnp.exp(a - m_new), r, jnp.exp(b_last + m_prev - m_new))):
                rows_ref[:, base + j, :] = val

    cf_ref[...] = jnp.zeros(cf_ref.shape, F32)
    cb_ref[...] = jnp.zeros(cb_ref.shape, F32)

    chains = [(hh, direction) for hh in range(nh) for direction in range(2)]
    c_refs = (cf_ref, cb_ref)
    h_refs = (hf_ref, hb_ref)

    def body(i, carry):
        chunks = (i, N_CHUNKS - 1 - i)
        tiles = (rows_ref[chunks[0]], rows_ref[chunks[1]])

        def row(n, j):
            hh, direction = chains[n]
            base = (2 * hh + direction) * ROWS_PER_CHAIN + j
            return tiles[direction][base:base + 1, :]

        r_rows = jnp.concatenate([row(n, 4) for n in range(len(chains))], axis=0)
        r_cols = sum(_dot_nt(eye, p) for p in _split3(r_rows))
        kcs, qtcs, vtas, kqs = [], [], [], []
        for hh, direction in chains:
            c = chunks[direction]
            r0 = pl.multiple_of(c * L, L)
            kcs.append(ks_ref[hh, pl.ds(r0, L), :])
            qtcs.append(qt_ref[hh, c])
            vtas.append(vt_ref[hh, c])
            kqs.append(_dot(kcs[-1], qtcs[-1]))
        rhss = []
        for n, (hh, direction) in enumerate(chains):
            st = kqs[n] * jnp.exp(jnp.where(masks[direction], r_cols[:, n:n + 1] + row(n, 0), -jnp.inf))
            wq = qtcs[n].astype(F32) * row(n, 1)
            rhss.append(jnp.concatenate([st.astype(BF16), wq.astype(BF16)], axis=0))
        states = [c_refs[direction][hh] for hh, direction in chains]
        nds = [_dot(jnp.concatenate([vtas[n], states[n].astype(BF16)], axis=1), rhss[n])
               for n in range(len(chains))]
        for n, (hh, direction) in enumerate(chains):
            nd = nds[n]
            h_refs[direction][hh, chunks[direction]] = (
                nd[:d, :] * (1.0 / jnp.maximum(jnp.abs(nd[d:d + 1, :]), row(n, 2))))
        for n, (hh, direction) in enumerate(chains):
            vw = (vtas[n].astype(F32) * row(n, 3)).astype(BF16)
            c_refs[direction][hh] = row(n, 5) * states[n] + _dot(vw, kcs[n])
        return carry

    lax.fori_loop(0, N_CHUNKS, body, 0)

    def finish(c, carry):
        r0 = pl.multiple_of(c * L, L)
        for hh in range(nh):
            sl = slice(hh * d, (hh + 1) * d)
            h = _rms((hf_ref[hh, c] + hb_ref[hh, c]).T, ng_ref[:, sl])
            o = o_ref[pl.ds(r0, L), sl].astype(F32)
            out_ref[pl.ds(r0, L), sl] = (h * (1.0 / (1.0 + jnp.exp(-o)))).astype(BF16)
        return carry

    lax.fori_loop(0, N_CHUNKS, finish, 0)


def _mlstm(proj, gt, gbias, conv_w, norm_g, batch):
    t = proj.shape[0]
    col = lambda off: pl.BlockSpec((SEQ, ML_WIDTH), lambda b: (b, off // ML_WIDTH))
    return pl.pallas_call(
        _mlstm_kernel,
        grid=(batch,),
        in_specs=[
            col(0), col(ML_WIDTH), col(2 * ML_WIDTH), col(3 * ML_WIDTH),
            pl.BlockSpec((ML_HEADS, N_CHUNKS, GATE_ROWS, LANES), lambda b: (0, b, 0, 0)),
            pl.BlockSpec((ML_HEADS, GATE_ROWS, 1), lambda b: (0, 0, 0)),
            pl.BlockSpec((3, ML_WIDTH), lambda b: (0, 0)),
            pl.BlockSpec((3, ML_WIDTH), lambda b: (0, 1)),
            pl.BlockSpec((1, ML_WIDTH), lambda b: (0, 0)),
        ],
        out_specs=pl.BlockSpec((SEQ, ML_WIDTH), lambda b: (b, 0)),
        out_shape=jax.ShapeDtypeStruct((t, ML_WIDTH), BF16),
        scratch_shapes=[
            pltpu.VMEM((ML_HEADS, N_CHUNKS, ML_HEAD_DIM, ML_CHUNK), BF16),
            pltpu.VMEM((ML_HEADS, SEQ, ML_HEAD_DIM), BF16),
            pltpu.VMEM((ML_HEADS, N_CHUNKS, AUG_ROWS, ML_CHUNK), BF16),
            pltpu.VMEM((ML_HEADS, N_CHUNKS, ML_HEAD_DIM, ML_CHUNK), F32),
            pltpu.VMEM((ML_HEADS, N_CHUNKS, ML_HEAD_DIM, ML_CHUNK), F32),
            pltpu.VMEM((ML_HEADS, AUG_ROWS, ML_HEAD_DIM), F32),
            pltpu.VMEM((ML_HEADS, AUG_ROWS, ML_HEAD_DIM), F32),
            pltpu.VMEM((N_CHUNKS, N_CHAIN_ROWS, LANES), F32),
        ],
        compiler_params=pltpu.CompilerParams(
            dimension_semantics=("arbitrary",), vmem_limit_bytes=VMEM_LIMIT),
        name="mlstm",
    )(proj, proj, proj, proj, gt, gbias, conv_w, conv_w, norm_g)


def _ffn_kernel(att_ref, ml_ref, x_ref, wo_ref, g2_ref, wup_ref, wdn_ref, gf_ref, out_ref):
    x1 = (x_ref[...] + _dot(att_ref[...], wo_ref[:ATT_WIDTH, :])
          + _dot(ml_ref[...], wo_ref[ATT_WIDTH:, :]))
    hn = _rms(x1, g2_ref[...]).astype(BF16)
    out_ref[...] = x1
    for c0 in range(0, D_FF, FF_CHUNK):
        hid = _dot(hn, wup_ref[:, c0:c0 + FF_CHUNK])
        act = jnp.square(jnp.maximum(hid, 0.0)).astype(BF16)
        out_ref[...] += _dot(act, wdn_ref[c0:c0 + FF_CHUNK, :])
    out_ref[...] = _rms(out_ref[...], gf_ref[...])


def _ffn(att, ml, x2, w_out, g2, w_up, w_down, gf):
    t = x2.shape[0]
    row = lambda w: pl.BlockSpec((ROW_TILE, w), lambda i: (i, 0))
    whole = lambda a: pl.BlockSpec(a.shape, lambda i: (0, 0), pipeline_mode=pl.Buffered(1))
    return pl.pallas_call(
        _ffn_kernel,
        grid=(t // ROW_TILE,),
        in_specs=[row(ATT_WIDTH), row(ML_WIDTH), row(D_MODEL), whole(w_out), whole(g2),
                  whole(w_up), whole(w_down), whole(gf)],
        out_specs=row(D_MODEL),
        out_shape=jax.ShapeDtypeStruct((t, D_MODEL), F32),
        compiler_params=pltpu.CompilerParams(
            dimension_semantics=("arbitrary",), vmem_limit_bytes=VMEM_LIMIT),
        name="outproj_ffn",
    )(att, ml, x2, w_out, g2, w_up, w_down, gf)


def _t5_bucket(rel):
    nb = REL_BUCKETS // 2
    max_exact = nb // 2
    ret = jnp.where(rel > 0, nb, 0)
    n = jnp.abs(rel)
    nf = jnp.maximum(n, 1).astype(jnp.float32)
    large = max_exact + (jnp.log(nf / max_exact) / math.log(REL_MAX_DIST / max_exact)
                         * (nb - max_exact)).astype(jnp.int32)
    large = jnp.minimum(large, nb - 1)
    return ret + jnp.where(n < max_exact, n, large)


def kernel(x, norm1_g, w_in, b_gates, conv_w, ml_norm_g, sink_logits, w_out, norm2_g, w_up, w_down,
           rel_bias, final_g):
    batch, seq, d_model = x.shape
    assert (seq, d_model) == (SEQ, D_MODEL) and w_in.shape[0] == 1
    x2 = x.reshape(batch * seq, d_model)

    w = w_in[0]
    att_cols = ATT_WIDTH + 2 * ATT_KV_WIDTH
    gate_cols = w[:, att_cols + 4 * ML_WIDTH:].reshape(d_model, 4, ML_HEADS).transpose(0, 2, 1)
    gate_cols = jnp.pad(gate_cols, ((0, 0), (0, 0), (0, GATE_ROWS - 4))).reshape(d_model, -1)
    gate_cols = jnp.pad(gate_cols, ((0, 0), (0, LANES - gate_cols.shape[1])))
    w_pad = jnp.concatenate([w[:, att_cols:att_cols + 4 * ML_WIDTH], w[:, :att_cols], gate_cols],
                            axis=1).astype(BF16)
    gbias = jnp.pad(b_gates[0].reshape(4, ML_HEADS).T, ((0, 0), (0, GATE_ROWS - 4)))[..., None]

    proj, gt = _inproj(x2, norm1_g, w_pad)

    r = jnp.arange(BLOCK)[:, None]
    c = jnp.arange(3 * BLOCK)[None, :]
    bucket = _t5_bucket(c - BLOCK - r).astype(jnp.int32)
    bias = _bias_table(rel_bias.astype(F32), bucket)
    att = _attention(proj, bias, sink_logits[0].astype(F32), batch)

    ml = _mlstm(proj, gt, gbias.astype(F32), conv_w[0], ml_norm_g, batch)

    out = _ffn(att, ml, x2, w_out[0].astype(BF16), norm2_g, w_up[0].astype(BF16),
               w_down[0].astype(BF16), final_g.reshape(1, d_model))
    return out.reshape(batch, seq, d_model)
```

```python
import math

import jax
import jax.numpy as jnp
from jax import lax
from jax.experimental import pallas as pl
from jax.experimental.pallas import tpu as pltpu

D_MODEL = 1024
SEQ = 2048
ATT_HEADS = 8
ATT_KV_HEADS = 2
ATT_HEAD_DIM = 64
ATT_WIDTH = ATT_HEADS * ATT_HEAD_DIM
ATT_KV_WIDTH = ATT_KV_HEADS * ATT_HEAD_DIM
WINDOW = 128
BLOCK = 128
REL_BUCKETS = 32
REL_MAX_DIST = 128
ML_HEADS = 4
ML_HEAD_DIM = 128
ML_WIDTH = ML_HEADS * ML_HEAD_DIM
ML_CHUNK = 128
N_GATE_COLS = 4 * ML_HEADS
D_FF = 4 * D_MODEL
EPS = 1e-6

LANES = 128
N_CHUNKS = SEQ // ML_CHUNK
GATE_ROWS = 8
MAIN_WIDTH = 4 * ML_WIDTH + 2 * ATT_KV_WIDTH
PROJ_PAD = MAIN_WIDTH + LANES
OFF_KA = 4 * ML_WIDTH
OFF_VA = OFF_KA + ATT_KV_WIDTH

ROW_TILE = 512
FF_CHUNK = 1024
VMEM_LIMIT = 56 * 1024 * 1024

F32 = jnp.float32
BF16 = jnp.bfloat16
NT_DIMS = (((1,), (1,)), ((), ()))


def _dot(a, b):
    return jnp.dot(a, b, preferred_element_type=F32)


def _dot_nt(a, b):
    return lax.dot_general(a, b, NT_DIMS, preferred_element_type=F32)


def _rms(x, g):
    ms = jnp.mean(x * x, axis=-1, keepdims=True)
    return x * lax.rsqrt(ms + EPS) * g


def _inproj_kernel(x_ref, g_ref, w_ref, wqt_ref, proj_ref, qt_ref, gt_ref):
    u = _rms(x_ref[...], g_ref[...]).astype(BF16)
    for c0 in range(0, MAIN_WIDTH, 512):
        cw = min(512, MAIN_WIDTH - c0)
        proj_ref[:, c0:c0 + cw] = _dot(u, w_ref[:, c0:c0 + cw]).astype(BF16)
    qt = (_dot_nt(wqt_ref[...], u) * (ATT_HEAD_DIM ** -0.5 * LOG2_E)).astype(BF16)
    for cc in range(ROW_TILE // BLOCK):
        qt_ref[cc] = qt[:, cc * BLOCK:(cc + 1) * BLOCK]
    gates = _dot(u, w_ref[:, MAIN_WIDTH:PROJ_PAD])
    for cc in range(ROW_TILE // ML_CHUNK):
        gt = gates[cc * ML_CHUNK:(cc + 1) * ML_CHUNK, :].T
        for h in range(ML_HEADS):
            gt_ref[h, cc] = gt[h * GATE_ROWS:(h + 1) * GATE_ROWS, :]


def _inproj(x2, g1, w_pad, wq_t):
    t = x2.shape[0]
    return pl.pallas_call(
        _inproj_kernel,
        grid=(t // ROW_TILE,),
        in_specs=[
            pl.BlockSpec((ROW_TILE, D_MODEL), lambda i: (i, 0)),
            pl.BlockSpec((1, D_MODEL), lambda i: (0, 0)),
            pl.BlockSpec((D_MODEL, PROJ_PAD), lambda i: (0, 0), pipeline_mode=pl.Buffered(1)),
            pl.BlockSpec((ATT_WIDTH, D_MODEL), lambda i: (0, 0), pipeline_mode=pl.Buffered(1)),
        ],
        out_specs=[
            pl.BlockSpec((ROW_TILE, MAIN_WIDTH), lambda i: (i, 0)),
            pl.BlockSpec((ROW_TILE // BLOCK, ATT_WIDTH, BLOCK), lambda i: (i, 0, 0)),
            pl.BlockSpec((ML_HEADS, ROW_TILE // ML_CHUNK, GATE_ROWS, LANES), lambda i: (0, i, 0, 0)),
        ],
        out_shape=[
            jax.ShapeDtypeStruct((t, MAIN_WIDTH), BF16),
            jax.ShapeDtypeStruct((t // BLOCK, ATT_WIDTH, BLOCK), BF16),
            jax.ShapeDtypeStruct((ML_HEADS, t // ML_CHUNK, GATE_ROWS, LANES), F32),
        ],
        compiler_params=pltpu.CompilerParams(
            dimension_semantics=("arbitrary",), vmem_limit_bytes=VMEM_LIMIT),
        name="inproj",
    )(x2, g1, w_pad, wq_t)


def _bias_kernel(rb_ref, bucket_ref, out_ref):
    bucket = bucket_ref[...]
    c = lax.broadcasted_iota(jnp.int32, bucket.shape, 0)
    r = lax.broadcasted_iota(jnp.int32, bucket.shape, 1)
    valid = jnp.abs(c - BLOCK - r) <= WINDOW
    for h in range(ATT_HEADS):
        acc = jnp.zeros(bucket.shape, F32)
        for b in range(REL_BUCKETS):
            acc = jnp.where(bucket == b, rb_ref[b, h], acc)
        out_ref[h] = jnp.where(valid, acc * LOG2_E, -jnp.inf)


def _bias_table(rel_bias, bucket_t):
    return pl.pallas_call(
        _bias_kernel,
        in_specs=[
            pl.BlockSpec(memory_space=pltpu.SMEM),
            pl.BlockSpec(memory_space=pltpu.VMEM),
        ],
        out_specs=pl.BlockSpec(memory_space=pltpu.VMEM),
        out_shape=jax.ShapeDtypeStruct((ATT_HEADS, 3 * BLOCK, BLOCK), F32),
        name="bias_table",
    )(rel_bias, bucket_t)


N_QBLOCKS = SEQ // BLOCK
ATT_SLOT_HEADS = tuple(p // 2 + 4 * (p % 2) for p in range(ATT_HEADS))


BAND = 3 * BLOCK
ATT_AUG_ROWS = LANES + 16
LOG2_E = math.log2(math.e)


def _attn_kernel(sink_ref, qt_ref, k_ref, v_ref, bias_ref, out_ref,
                 kz_ref, vtz_ref, sa_ref, sb_ref, pa_ref, pb_ref, la_ref, lb_ref):
    half = ATT_HEAD_DIM
    kf = k_ref[...].astype(F32)
    lane_s = lax.broadcasted_iota(jnp.int32, kf.shape, 1)
    zero_blk = jnp.zeros((BLOCK, LANES), BF16)
    row_t = lax.broadcasted_iota(jnp.int32, (ATT_AUG_ROWS - LANES, BLOCK), 0)
    ones_rows = [jnp.where(row_t == kv, 1.0, 0.0).astype(BF16) for kv in range(2)]
    for kv in range(2):
        kz_ref[kv, 0:BLOCK, :] = zero_blk
        kz_ref[kv, SEQ + BLOCK:SEQ + 2 * BLOCK, :] = zero_blk
        vtz_ref[kv, 0] = jnp.concatenate([zero_blk, ones_rows[kv]], axis=0)
        vtz_ref[kv, N_QBLOCKS + 1] = jnp.concatenate([zero_blk, ones_rows[kv]], axis=0)
    kz_ref[0, BLOCK:SEQ + BLOCK, :] = jnp.where(lane_s < half, kf, 0.0).astype(BF16)
    kz_ref[1, BLOCK:SEQ + BLOCK, :] = jnp.where(lane_s < half, 0.0, kf).astype(BF16)
    row_b = lax.broadcasted_iota(jnp.int32, (LANES, BLOCK), 0)
    for jb in range(N_QBLOCKS):
        vt = v_ref[jb * BLOCK:(jb + 1) * BLOCK, :].astype(F32).T
        vtz_ref[0, jb + 1] = jnp.concatenate([jnp.where(row_b < half, vt, 0.0).astype(BF16), ones_rows[0]], axis=0)
        vtz_ref[1, jb + 1] = jnp.concatenate([jnp.where(row_b < half, 0.0, vt).astype(BF16), ones_rows[1]], axis=0)
    row_o = lax.broadcasted_iota(jnp.int32, (LANES, 2 * BLOCK), 0)
    neg_blk = jnp.full((BLOCK, BLOCK), -jnp.inf, F32)

    def start(j):
        return j * BLOCK if isinstance(j, int) else pl.multiple_of(j * BLOCK, BLOCK)

    def stage_scores(j, s_ref):
        qt = qt_ref[j]
        k0 = start(j)
        kzb = jnp.concatenate([kz_ref[0, pl.ds(k0, BAND), :], kz_ref[1, pl.ds(k0, BAND), :]], axis=0)
        for pair in range(2):
            rows = 2 * LANES * pair
            rhs = jnp.concatenate([qt[rows:rows + LANES, :], qt[rows + LANES:rows + 2 * LANES, :]], axis=1)
            s_ref[pair] = _dot(kzb, rhs)

    def stage_softmax(s_ref, p_ref, l_ref, edge):
        for pair in range(2):
            for kv in range(2):
                for t in range(2):
                    h = 2 * pair + t + 4 * kv
                    rs = slice(kv * BAND, (kv + 1) * BAND)
                    cs = slice(t * BLOCK, (t + 1) * BLOCK)
                    if edge < 0:
                        bias = jnp.concatenate([neg_blk, bias_ref[h, BLOCK:BAND, :]], axis=0)
                    elif edge > 0:
                        bias = jnp.concatenate([bias_ref[h, 0:2 * BLOCK, :], neg_blk], axis=0)
                    else:
                        bias = bias_ref[h]
                    s = s_ref[pair, rs, cs] + bias
                    sink = sink_ref[h] * LOG2_E
                    m = jnp.maximum(jnp.max(s, axis=0, keepdims=True), sink)
                    l_ref[pair, kv:kv + 1, cs] = jnp.exp2(sink - m)
                    p_ref[pair, rs, cs] = jnp.exp2(s - m).astype(BF16)

    def stage_pv(j, p_ref, l_ref):
        q0 = start(j)
        vtb = jnp.concatenate([vtz_ref[kv, j + i] for kv in range(2) for i in range(3)], axis=1)
        for pair in range(2):
            ot = _dot(vtb, p_ref[pair])
            inv = 1.0 / (ot[LANES:LANES + 2, :] + l_ref[pair, 0:2, :])
            ot = ot[:LANES, :] * jnp.where(row_o < half, inv[0:1, :], inv[1:2, :])
            for t in range(2):
                c0 = (2 * pair + t) * LANES
                out_ref[pl.ds(q0, BLOCK), c0:c0 + LANES] = ot[:, t * BLOCK:(t + 1) * BLOCK].T.astype(BF16)

    stage_scores(0, sa_ref)
    stage_scores(1, sb_ref)
    stage_softmax(sa_ref, pa_ref, la_ref, -1)

    def two_blocks(ii, carry):
        j = 2 * ii
        stage_pv(j - 2, pa_ref, la_ref)
        stage_scores(j, sa_ref)
        stage_softmax(sb_ref, pb_ref, lb_ref, 0)
        stage_pv(j - 1, pb_ref, lb_ref)
        stage_scores(j + 1, sb_ref)
        stage_softmax(sa_ref, pa_ref, la_ref, 0)
        return carry

    lax.fori_loop(1, N_QBLOCKS // 2, two_blocks, 0)
    stage_softmax(sb_ref, pb_ref, lb_ref, 1)
    stage_pv(N_QBLOCKS - 2, pa_ref, la_ref)
    stage_pv(N_QBLOCKS - 1, pb_ref, lb_ref)


def _attention(qt, proj, bias, sink, batch):
    t = proj.shape[0]
    kv_spec = lambda off: pl.BlockSpec((SEQ, ATT_KV_WIDTH), lambda b: (b, off // ATT_KV_WIDTH))
    return pl.pallas_call(
        _attn_kernel,
        grid=(batch,),
        in_specs=[
            pl.BlockSpec(memory_space=pltpu.SMEM),
            pl.BlockSpec((N_QBLOCKS, ATT_WIDTH, BLOCK), lambda b: (b, 0, 0)),
            kv_spec(OFF_KA),
            kv_spec(OFF_VA),
            pl.BlockSpec((ATT_HEADS, 3 * BLOCK, BLOCK), lambda b: (0, 0, 0)),
        ],
        out_specs=pl.BlockSpec((SEQ, ATT_WIDTH), lambda b: (b, 0)),
        out_shape=jax.ShapeDtypeStruct((t, ATT_WIDTH), BF16),
        scratch_shapes=[
            pltpu.VMEM((2, SEQ + 2 * BLOCK, ATT_KV_WIDTH), BF16),
            pltpu.VMEM((2, N_QBLOCKS + 2, ATT_AUG_ROWS, BLOCK), BF16),
            pltpu.VMEM((2, 2 * BAND, 2 * BLOCK), F32),
            pltpu.VMEM((2, 2 * BAND, 2 * BLOCK), F32),
            pltpu.VMEM((2, 2 * BAND, 2 * BLOCK), BF16),
            pltpu.VMEM((2, 2 * BAND, 2 * BLOCK), BF16),
            pltpu.VMEM((2, 8, 2 * BLOCK), F32),
            pltpu.VMEM((2, 8, 2 * BLOCK), F32),
        ],
        compiler_params=pltpu.CompilerParams(
            dimension_semantics=("arbitrary",), vmem_limit_bytes=VMEM_LIMIT),
        name="attention",
    )(sink, qt, proj, proj, bias)


def _log_sigmoid(x):
    return jnp.minimum(x, 0.0) - jnp.log1p(jnp.exp(-jnp.abs(x)))


def _split3(x):
    hi = x.astype(BF16)
    r1 = x - hi.astype(F32)
    mid = r1.astype(BF16)
    lo = (r1 - mid.astype(F32)).astype(BF16)
    return hi, mid, lo


AUG_ROWS = ML_HEAD_DIM + 16
ROWS_PER_CHAIN = 6
N_CHAIN_ROWS = 2 * ML_HEADS * ROWS_PER_CHAIN


def _mlstm_kernel(q_ref, k_ref, v_ref, o_ref, gt_ref, gb_ref, cwq_ref, cwk_ref, ng_ref, out_ref,
                  qt_ref, ks_ref, vt_ref, hf_ref, hb_ref, cf_ref, cb_ref, rows_ref):
    L = ML_CHUNK
    d = ML_HEAD_DIM
    nh = ML_HEADS
    si = lax.broadcasted_iota(jnp.int32, (L, L), 0)
    ti = lax.broadcasted_iota(jnp.int32, (L, L), 1)
    masks = (si <= ti, si >= ti)
    ones_le = jnp.where(si <= ti, 1.0, 0.0).astype(BF16)
    ones_ge = jnp.where(si >= ti, 1.0, 0.0).astype(BF16)
    eye = jnp.where(si == ti, 1.0, 0.0).astype(BF16)
    aug_tail = jnp.where(lax.broadcasted_iota(jnp.int32, (AUG_ROWS - d, L), 0) == 0, 1.0, 0.0).astype(BF16)
    k_scale = ML_HEAD_DIM ** -0.5

    def prep(c, carry):
        r0 = pl.multiple_of(c * L, L)
        rp = pl.multiple_of(jnp.maximum(r0 - 16, 0), 16)
        rn = pl.multiple_of(jnp.minimum(r0 + L, SEQ - 16), 16)
        first = c == 0
        last = c == N_CHUNKS - 1
        for hh in range(nh):
            sl = slice(hh * d, (hh + 1) * d)
            for src, cw_ref, is_k in ((q_ref, cwq_ref, False), (k_ref, cwk_ref, True)):
                cur = src[pl.ds(r0, L), sl].astype(F32)
                halo_p = src[pl.ds(rp, 16), sl].astype(F32)[15:16, :]
                halo_n = src[pl.ds(rn, 16), sl].astype(F32)[0:1, :]
                halo_p = jnp.where(first, 0.0, halo_p)
                halo_n = jnp.where(last, 0.0, halo_n)
                prev = jnp.where(si == 0, halo_p, pltpu.roll(cur, 1, 0))
                nxt = jnp.where(si == L - 1, halo_n, pltpu.roll(cur, L - 1, 0))
                w = cw_ref[:, sl]
                y = w[0:1, :] * prev + w[1:2, :] * cur + w[2:3, :] * nxt
                y = y * (1.0 / (1.0 + jnp.exp(-y)))
                if is_k:
                    ks_ref[hh, pl.ds(r0, L), :] = (y * k_scale).astype(BF16)
                else:
                    qt_ref[hh, c] = y.T.astype(BF16)
            vt = v_ref[pl.ds(r0, L), sl].astype(F32).T.astype(BF16)
            vt_ref[hh, c] = jnp.concatenate([vt, aug_tail], axis=0)
        return carry

    lax.fori_loop(0, N_CHUNKS, prep, 0)

    lane16 = lax.broadcasted_iota(jnp.int32, (N_CHUNKS, L), 1)
    row16 = lax.broadcasted_iota(jnp.int32, (N_CHUNKS, L), 0)
    for hh in range(nh):
        for direction in range(2):
            fwd = direction == 0
            k_i, k_f = 2 * direction, 2 * direction + 1
            li = gt_ref[hh, :, k_i, :] + gb_ref[hh, k_i:k_i + 1, :]
            lf = _log_sigmoid(gt_ref[hh, :, k_f, :] + gb_ref[hh, k_f:k_f + 1, :])
            ones = ones_le if fwd else ones_ge
            b = sum(_dot(p, ones) for p in _split3(lf))
            b_last = jnp.broadcast_to(b[:, L - 1:L] if fwd else b[:, 0:1], (N_CHUNKS, L))
            a = b_last - b + li
            a_max = jnp.broadcast_to(jnp.max(a, axis=-1, keepdims=True), (N_CHUNKS, L))
            r = li - b
            cm = r
            for sh in (1, 2, 4, 8, 16, 32, 64):
                if fwd:
                    cm = jnp.maximum(cm, jnp.where(lane16 >= sh, pltpu.roll(cm, sh, 1), -jnp.inf))
                else:
                    cm = jnp.maximum(cm, jnp.where(lane16 < L - sh, pltpu.roll(cm, L - sh, 1), -jnp.inf))
            m_cur = jnp.zeros((1, L), F32)
            m_prev = jnp.zeros((N_CHUNKS, L), F32)
            m_new = jnp.zeros((N_CHUNKS, L), F32)
            for c in (range(N_CHUNKS) if fwd else reversed(range(N_CHUNKS))):
                m_prev = jnp.where(row16 == c, m_cur, m_prev)
                m_cur = jnp.maximum(b_last[c:c + 1, :] + m_cur, a_max[c:c + 1, :])
                m_new = jnp.where(row16 == c, m_cur, m_new)
            m_t = b + jnp.maximum(m_prev, cm)
            base = (2 * hh + direction) * ROWS_PER_CHAIN
            for j, val in enumerate((b - m_t, jnp.exp(b + m_prev - m_t), jnp.exp(-m_t),
                                     jnp.exp(a - m_new), r, jnp.exp(b_last + m_prev - m_new))):
                rows_ref[:, base + j, :] = val

    cf_ref[...] = jnp.zeros(cf_ref.shape, F32)
    cb_ref[...] = jnp.zeros(cb_ref.shape, F32)

    chains = [(hh, direction) for hh in range(nh) for direction in range(2)]
    c_refs = (cf_ref, cb_ref)
    h_refs = (hf_ref, hb_ref)

    def body(i, carry):
        chunks = (i, N_CHUNKS - 1 - i)
        tiles = (rows_ref[chunks[0]], rows_ref[chunks[1]])

        def row(n, j):
            hh, direction = chains[n]
            base = (2 * hh + direction) * ROWS_PER_CHAIN + j
            return tiles[direction][base:base + 1, :]

        r_rows = jnp.concatenate([row(n, 4) for n in range(len(chains))], axis=0)
        r_cols = sum(_dot_nt(eye, p) for p in _split3(r_rows))
        kcs, qtcs, vtas, kqs = [], [], [], []
        for hh, direction in chains:
            c = chunks[direction]
            r0 = pl.multiple_of(c * L, L)
            kcs.append(ks_ref[hh, pl.ds(r0, L), :])
            qtcs.append(qt_ref[hh, c])
            vtas.append(vt_ref[hh, c])
            kqs.append(_dot(kcs[-1], qtcs[-1]))
        rhss = []
        for n, (hh, direction) in enumerate(chains):
            st = kqs[n] * jnp.exp(jnp.where(masks[direction], r_cols[:, n:n + 1] + row(n, 0), -jnp.inf))
            wq = qtcs[n].astype(F32) * row(n, 1)
            rhss.append(jnp.concatenate([st.astype(BF16), wq.astype(BF16)], axis=0))
        states = [c_refs[direction][hh] for hh, direction in chains]
        nds = [_dot(jnp.concatenate([vtas[n], states[n].astype(BF16)], axis=1), rhss[n])
               for n in range(len(chains))]
        for n, (hh, direction) in enumerate(chains):
            nd = nds[n]
            h_refs[direction][hh, chunks[direction]] = (
                nd[:d, :] * (1.0 / jnp.maximum(jnp.abs(nd[d:d + 1, :]), row(n, 2))))
        for n, (hh, direction) in enumerate(chains):
            vw = (vtas[n].astype(F32) * row(n, 3)).astype(BF16)
            c_refs[direction][hh] = row(n, 5) * states[n] + _dot(vw, kcs[n])
        return carry

    lax.fori_loop(0, N_CHUNKS, body, 0)

    def finish(c, carry):
        r0 = pl.multiple_of(c * L, L)
        for hh in range(nh):
            sl = slice(hh * d, (hh + 1) * d)
            h = _rms((hf_ref[hh, c] + hb_ref[hh, c]).T, ng_ref[:, sl])
            o = o_ref[pl.ds(r0, L), sl].astype(F32)
            out_ref[pl.ds(r0, L), sl] = (h * (1.0 / (1.0 + jnp.exp(-o)))).astype(BF16)
        return carry

    lax.fori_loop(0, N_CHUNKS, finish, 0)


def _mlstm(proj, gt, gbias, conv_w, norm_g, batch):
    t = proj.shape[0]
    col = lambda off: pl.BlockSpec((SEQ, ML_WIDTH), lambda b: (b, off // ML_WIDTH))
    return pl.pallas_call(
        _mlstm_kernel,
        grid=(batch,),
        in_specs=[
            col(0), col(ML_WIDTH), col(2 * ML_WIDTH), col(3 * ML_WIDTH),
            pl.BlockSpec((ML_HEADS, N_CHUNKS, GATE_ROWS, LANES), lambda b: (0, b, 0, 0)),
            pl.BlockSpec((ML_HEADS, GATE_ROWS, 1), lambda b: (0, 0, 0)),
            pl.BlockSpec((3, ML_WIDTH), lambda b: (0, 0)),
            pl.BlockSpec((3, ML_WIDTH), lambda b: (0, 1)),
            pl.BlockSpec((1, ML_WIDTH), lambda b: (0, 0)),
        ],
        out_specs=pl.BlockSpec((SEQ, ML_WIDTH), lambda b: (b, 0)),
        out_shape=jax.ShapeDtypeStruct((t, ML_WIDTH), BF16),
        scratch_shapes=[
            pltpu.VMEM((ML_HEADS, N_CHUNKS, ML_HEAD_DIM, ML_CHUNK), BF16),
            pltpu.VMEM((ML_HEADS, SEQ, ML_HEAD_DIM), BF16),
            pltpu.VMEM((ML_HEADS, N_CHUNKS, AUG_ROWS, ML_CHUNK), BF16),
            pltpu.VMEM((ML_HEADS, N_CHUNKS, ML_HEAD_DIM, ML_CHUNK), F32),
            pltpu.VMEM((ML_HEADS, N_CHUNKS, ML_HEAD_DIM, ML_CHUNK), F32),
            pltpu.VMEM((ML_HEADS, AUG_ROWS, ML_HEAD_DIM), F32),
            pltpu.VMEM((ML_HEADS, AUG_ROWS, ML_HEAD_DIM), F32),
            pltpu.VMEM((N_CHUNKS, N_CHAIN_ROWS, LANES), F32),
        ],
        compiler_params=pltpu.CompilerParams(
            dimension_semantics=("arbitrary",), vmem_limit_bytes=VMEM_LIMIT),
        name="mlstm",
    )(proj, proj, proj, proj, gt, gbias, conv_w, conv_w, norm_g)


def _ffn_kernel(att_ref, ml_ref, x_ref, wo_ref, g2_ref, wup_ref, wdn_ref, gf_ref, out_ref):
    x1 = (x_ref[...] + _dot(att_ref[...], wo_ref[:ATT_WIDTH, :])
          + _dot(ml_ref[...], wo_ref[ATT_WIDTH:, :]))
    hn = _rms(x1, g2_ref[...]).astype(BF16)
    out_ref[...] = x1
    for c0 in range(0, D_FF, FF_CHUNK):
        hid = _dot(hn, wup_ref[:, c0:c0 + FF_CHUNK])
        act = jnp.square(jnp.maximum(hid, 0.0)).astype(BF16)
        out_ref[...] += _dot(act, wdn_ref[c0:c0 + FF_CHUNK, :])
    out_ref[...] = _rms(out_ref[...], gf_ref[...])


def _ffn(att, ml, x2, w_out, g2, w_up, w_down, gf):
    t = x2.shape[0]
    row = lambda w: pl.BlockSpec((ROW_TILE, w), lambda i: (i, 0))
    whole = lambda a: pl.BlockSpec(a.shape, lambda i: (0, 0), pipeline_mode=pl.Buffered(1))
    return pl.pallas_call(
        _ffn_kernel,
        grid=(t // ROW_TILE,),
        in_specs=[row(ATT_WIDTH), row(ML_WIDTH), row(D_MODEL), whole(w_out), whole(g2),
                  whole(w_up), whole(w_down), whole(gf)],
        out_specs=row(D_MODEL),
        out_shape=jax.ShapeDtypeStruct((t, D_MODEL), F32),
        compiler_params=pltpu.CompilerParams(
            dimension_semantics=("arbitrary",), vmem_limit_bytes=VMEM_LIMIT),
        name="outproj_ffn",
    )(att, ml, x2, w_out, g2, w_up, w_down, gf)


def _t5_bucket(rel):
    nb = REL_BUCKETS // 2
    max_exact = nb // 2
    ret = jnp.where(rel > 0, nb, 0)
    n = jnp.abs(rel)
    nf = jnp.maximum(n, 1).astype(jnp.float32)
    large = max_exact + (jnp.log(nf / max_exact) / math.log(REL_MAX_DIST / max_exact)
                         * (nb - max_exact)).astype(jnp.int32)
    large = jnp.minimum(large, nb - 1)
    return ret + jnp.where(n < max_exact, n, large)


def kernel(x, norm1_g, w_in, b_gates, conv_w, ml_norm_g, sink_logits, w_out, norm2_g, w_up, w_down,
           rel_bias, final_g):
    batch, seq, d_model = x.shape
    assert (seq, d_model) == (SEQ, D_MODEL) and w_in.shape[0] == 1
    x2 = x.reshape(batch * seq, d_model)

    w = w_in[0]
    att_cols = ATT_WIDTH + 2 * ATT_KV_WIDTH
    gate_cols = w[:, att_cols + 4 * ML_WIDTH:].reshape(d_model, 4, ML_HEADS).transpose(0, 2, 1)
    gate_cols = jnp.pad(gate_cols, ((0, 0), (0, 0), (0, GATE_ROWS - 4))).reshape(d_model, -1)
    gate_cols = jnp.pad(gate_cols, ((0, 0), (0, LANES - gate_cols.shape[1])))
    w_pad = jnp.concatenate([w[:, att_cols:att_cols + 4 * ML_WIDTH], w[:, ATT_WIDTH:att_cols], gate_cols],
                            axis=1).astype(BF16)
    slots = jnp.asarray(ATT_SLOT_HEADS)
    wq_t = w[:, :ATT_WIDTH].reshape(d_model, ATT_HEADS, ATT_HEAD_DIM)[:, slots, :]
    wq_t = wq_t.reshape(d_model, ATT_WIDTH).T.astype(BF16)
    gbias = jnp.pad(b_gates[0].reshape(4, ML_HEADS).T, ((0, 0), (0, GATE_ROWS - 4)))[..., None]

    proj, qt, gt = _inproj(x2, norm1_g, w_pad, wq_t)

    c = jnp.arange(3 * BLOCK)[:, None]
    r = jnp.arange(BLOCK)[None, :]
    bucket_t = _t5_bucket(c - BLOCK - r).astype(jnp.int32)
    bias = _bias_table(rel_bias.astype(F32), bucket_t)
    att = _attention(qt, proj, bias, sink_logits[0].astype(F32), batch)

    ml = _mlstm(proj, gt, gbias.astype(F32), conv_w[0], ml_norm_g, batch)

    wo = w_out[0]
    wo_att = wo[:ATT_WIDTH].reshape(ATT_HEADS, ATT_HEAD_DIM, d_model)[slots].reshape(ATT_WIDTH, d_model)
    wo = jnp.concatenate([wo_att, wo[ATT_WIDTH:]], axis=0)
    out = _ffn(att, ml, x2, wo.astype(BF16), norm2_g, w_up[0].astype(BF16),
               w_down[0].astype(BF16), final_g.reshape(1, d_model))
    return out.reshape(batch, seq, d_model)
```

```python
import math

import jax
import jax.numpy as jnp
from jax import lax
from jax.experimental import pallas as pl
from jax.experimental.pallas import tpu as pltpu

D_MODEL = 1024
SEQ = 2048
ATT_HEADS = 8
ATT_KV_HEADS = 2
ATT_HEAD_DIM = 64
ATT_WIDTH = ATT_HEADS * ATT_HEAD_DIM
ATT_KV_WIDTH = ATT_KV_HEADS * ATT_HEAD_DIM
WINDOW = 128
BLOCK = 128
REL_BUCKETS = 32
REL_MAX_DIST = 128
ML_HEADS = 4
ML_HEAD_DIM = 128
ML_WIDTH = ML_HEADS * ML_HEAD_DIM
ML_CHUNK = 128
N_GATE_COLS = 4 * ML_HEADS
D_FF = 4 * D_MODEL
EPS = 1e-6

LANES = 128
N_CHUNKS = SEQ // ML_CHUNK
GATE_ROWS = 8
MAIN_WIDTH = 4 * ML_WIDTH + 2 * ATT_KV_WIDTH
PROJ_PAD = MAIN_WIDTH + LANES
OFF_OM = ML_WIDTH
OFF_KA = 2 * ML_WIDTH
OFF_VA = OFF_KA + ATT_KV_WIDTH
PROJ_OUT = OFF_VA + ATT_KV_WIDTH
HALO = 8
AUG_ROWS = ML_HEAD_DIM + 16

ROW_TILE = 512
FF_CHUNK = 1024
VMEM_LIMIT = 56 * 1024 * 1024

F32 = jnp.float32
BF16 = jnp.bfloat16
NT_DIMS = (((1,), (1,)), ((), ()))


def _dot(a, b):
    return jnp.dot(a, b, preferred_element_type=F32)


def _dot_nt(a, b):
    return lax.dot_general(a, b, NT_DIMS, preferred_element_type=F32)


def _rms(x, g):
    ms = jnp.mean(x * x, axis=-1, keepdims=True)
    return x * lax.rsqrt(ms + EPS) * g


def _inproj_kernel(x_ref, xp_ref, xn_ref, g_ref, w_ref, wqt_ref, cw_ref,
                   proj_ref, qt_ref, gt_ref, qmt_ref, vta_ref, aq_ref, ak_ref, av_ref):
    i = pl.program_id(0)
    tiles_per_seq = SEQ // ROW_TILE
    has_prev = (i % tiles_per_seq) != 0
    has_next = (i % tiles_per_seq) != tiles_per_seq - 1
    g = g_ref[...]
    u = _rms(x_ref[...], g).astype(BF16)
    u_halo = _rms(jnp.concatenate([xp_ref[...], xn_ref[...]], axis=0), g).astype(BF16)
    u_ext = jnp.concatenate([u, u_halo], axis=0)
    row = lax.broadcasted_iota(jnp.int32, (ROW_TILE, ML_WIDTH), 0)

    def conv_silu(acc, w):
        cur = acc[0:ROW_TILE, :]
        halo_p = jnp.where(has_prev, acc[ROW_TILE + HALO - 1:ROW_TILE + HALO, :], 0.0)
        halo_n = jnp.where(has_next, acc[ROW_TILE + HALO:ROW_TILE + HALO + 1, :], 0.0)
        prev = jnp.where(row == 0, halo_p, pltpu.roll(cur, 1, 0))
        nxt = jnp.where(row == ROW_TILE - 1, halo_n, pltpu.roll(cur, ROW_TILE - 1, 0))
        y = w[0:1, :] * prev + w[1:2, :] * cur + w[2:3, :] * nxt
        return y * (1.0 / (1.0 + jnp.exp(-y)))

    aug_tail = jnp.where(lax.broadcasted_iota(jnp.int32, (AUG_ROWS - ML_HEAD_DIM, ML_CHUNK), 0) == 0,
                         1.0, 0.0).astype(BF16)
    tiles = [(hh, cc) for hh in range(ML_HEADS) for cc in range(ROW_TILE // ML_CHUNK)]

    def tile_t(a, hh, cc):
        return a[cc * ML_CHUNK:(cc + 1) * ML_CHUNK, hh * ML_HEAD_DIM:(hh + 1) * ML_HEAD_DIM].T.astype(BF16)

    aq_ref[...] = _dot(u_ext, w_ref[:, 0:ML_WIDTH])
    ak_ref[...] = _dot(u_ext, w_ref[:, ML_WIDTH:2 * ML_WIDTH])
    q = conv_silu(aq_ref, cw_ref[:, 0:ML_WIDTH])
    for hh, cc in tiles:
        qmt_ref[hh, cc] = tile_t(q, hh, cc)
    av_ref[...] = _dot(u, w_ref[:, 2 * ML_WIDTH:3 * ML_WIDTH])
    k = conv_silu(ak_ref, cw_ref[:, ML_WIDTH:2 * ML_WIDTH])
    proj_ref[:, 0:ML_WIDTH] = (k * (ML_HEAD_DIM ** -0.5)).astype(BF16)
    proj_ref[:, ML_WIDTH:2 * ML_WIDTH] = _dot(u, w_ref[:, 3 * ML_WIDTH:4 * ML_WIDTH]).astype(BF16)
    for hh, cc in tiles:
        vta_ref[hh, cc] = jnp.concatenate([tile_t(av_ref, hh, cc), aug_tail], axis=0)
    proj_ref[:, 2 * ML_WIDTH:PROJ_OUT] = _dot(u, w_ref[:, 4 * ML_WIDTH:MAIN_WIDTH]).astype(BF16)
    qt = (_dot_nt(wqt_ref[...], u) * (ATT_HEAD_DIM ** -0.5 * LOG2_E)).astype(BF16)
    for cc in range(ROW_TILE // BLOCK):
        qt_ref[cc] = qt[:, cc * BLOCK:(cc + 1) * BLOCK]
    gates = _dot(u, w_ref[:, MAIN_WIDTH:PROJ_PAD])
    for cc in range(ROW_TILE // ML_CHUNK):
        gt = gates[cc * ML_CHUNK:(cc + 1) * ML_CHUNK, :].T
        for h in range(ML_HEADS):
            gt_ref[h, cc] = gt[h * GATE_ROWS:(h + 1) * GATE_ROWS, :]


def _inproj(x2, g1, w_pad, wq_t, conv_w):
    t = x2.shape[0]
    halo_blocks = ROW_TILE // HALO
    chunks = ROW_TILE // ML_CHUNK
    return pl.pallas_call(
        _inproj_kernel,
        grid=(t // ROW_TILE,),
        in_specs=[
            pl.BlockSpec((ROW_TILE, D_MODEL), lambda i: (i, 0)),
            pl.BlockSpec((HALO, D_MODEL), lambda i: (jnp.maximum(i * halo_blocks - 1, 0), 0)),
            pl.BlockSpec((HALO, D_MODEL), lambda i: (jnp.minimum((i + 1) * halo_blocks, t // HALO - 1), 0)),
            pl.BlockSpec((1, D_MODEL), lambda i: (0, 0)),
            pl.BlockSpec((D_MODEL, PROJ_PAD), lambda i: (0, 0), pipeline_mode=pl.Buffered(1)),
            pl.BlockSpec((ATT_WIDTH, D_MODEL), lambda i: (0, 0), pipeline_mode=pl.Buffered(1)),
            pl.BlockSpec((3, 2 * ML_WIDTH), lambda i: (0, 0)),
        ],
        out_specs=[
            pl.BlockSpec((ROW_TILE, PROJ_OUT), lambda i: (i, 0)),
            pl.BlockSpec((ROW_TILE // BLOCK, ATT_WIDTH, BLOCK), lambda i: (i, 0, 0)),
            pl.BlockSpec((ML_HEADS, chunks, GATE_ROWS, LANES), lambda i: (0, i, 0, 0)),
            pl.BlockSpec((ML_HEADS, chunks, ML_HEAD_DIM, ML_CHUNK), lambda i: (0, i, 0, 0)),
            pl.BlockSpec((ML_HEADS, chunks, AUG_ROWS, ML_CHUNK), lambda i: (0, i, 0, 0)),
        ],
        out_shape=[
            jax.ShapeDtypeStruct((t, PROJ_OUT), BF16),
            jax.ShapeDtypeStruct((t // BLOCK, ATT_WIDTH, BLOCK), BF16),
            jax.ShapeDtypeStruct((ML_HEADS, t // ML_CHUNK, GATE_ROWS, LANES), F32),
            jax.ShapeDtypeStruct((ML_HEADS, t // ML_CHUNK, ML_HEAD_DIM, ML_CHUNK), BF16),
            jax.ShapeDtypeStruct((ML_HEADS, t // ML_CHUNK, AUG_ROWS, ML_CHUNK), BF16),
        ],
        scratch_shapes=[
            pltpu.VMEM((ROW_TILE + 2 * HALO, ML_WIDTH), F32),
            pltpu.VMEM((ROW_TILE + 2 * HALO, ML_WIDTH), F32),
            pltpu.VMEM((ROW_TILE, ML_WIDTH), F32),
        ],
        compiler_params=pltpu.CompilerParams(
            dimension_semantics=("arbitrary",), vmem_limit_bytes=VMEM_LIMIT),
        name="inproj",
    )(x2, x2, x2, g1, w_pad, wq_t, conv_w)


def _bias_kernel(rb_ref, bucket_ref, out_ref):
    bucket = bucket_ref[...]
    c = lax.broadcasted_iota(jnp.int32, bucket.shape, 0)
    r = lax.broadcasted_iota(jnp.int32, bucket.shape, 1)
    valid = jnp.abs(c - BLOCK - r) <= WINDOW
    for h in range(ATT_HEADS):
        acc = jnp.zeros(bucket.shape, F32)
        for b in range(REL_BUCKETS):
            acc = jnp.where(bucket == b, rb_ref[b, h], acc)
        out_ref[h] = jnp.where(valid, acc * LOG2_E, -jnp.inf)


def _bias_table(rel_bias, bucket_t):
    return pl.pallas_call(
        _bias_kernel,
        in_specs=[
            pl.BlockSpec(memory_space=pltpu.SMEM),
            pl.BlockSpec(memory_space=pltpu.VMEM),
        ],
        out_specs=pl.BlockSpec(memory_space=pltpu.VMEM),
        out_shape=jax.ShapeDtypeStruct((ATT_HEADS, 3 * BLOCK, BLOCK), F32),
        name="bias_table",
    )(rel_bias, bucket_t)


N_QBLOCKS = SEQ // BLOCK
ATT_SLOT_HEADS = tuple(p // 2 + 4 * (p % 2) for p in range(ATT_HEADS))


BAND = 3 * BLOCK
ATT_AUG_ROWS = LANES + 16
LOG2_E = math.log2(math.e)


def _attn_kernel(sink_ref, qt_ref, k_ref, v_ref, bias_ref, out_ref,
                 kz_ref, vtz_ref, sa_ref, sb_ref, pa_ref, pb_ref, la_ref, lb_ref):
    half = ATT_HEAD_DIM
    kf = k_ref[...].astype(F32)
    lane_s = lax.broadcasted_iota(jnp.int32, kf.shape, 1)
    zero_blk = jnp.zeros((BLOCK, LANES), BF16)
    row_t = lax.broadcasted_iota(jnp.int32, (ATT_AUG_ROWS - LANES, BLOCK), 0)
    ones_rows = [jnp.where(row_t == kv, 1.0, 0.0).astype(BF16) for kv in range(2)]
    for kv in range(2):
        kz_ref[kv, 0:BLOCK, :] = zero_blk
        kz_ref[kv, SEQ + BLOCK:SEQ + 2 * BLOCK, :] = zero_blk
        vtz_ref[kv, 0] = jnp.concatenate([zero_blk, ones_rows[kv]], axis=0)
        vtz_ref[kv, N_QBLOCKS + 1] = jnp.concatenate([zero_blk, ones_rows[kv]], axis=0)
    kz_ref[0, BLOCK:SEQ + BLOCK, :] = jnp.where(lane_s < half, kf, 0.0).astype(BF16)
    kz_ref[1, BLOCK:SEQ + BLOCK, :] = jnp.where(lane_s < half, 0.0, kf).astype(BF16)
    row_b = lax.broadcasted_iota(jnp.int32, (LANES, BLOCK), 0)
    for jb in range(N_QBLOCKS):
        vt = v_ref[jb * BLOCK:(jb + 1) * BLOCK, :].astype(F32).T
        vtz_ref[0, jb + 1] = jnp.concatenate([jnp.where(row_b < half, vt, 0.0).astype(BF16), ones_rows[0]], axis=0)
        vtz_ref[1, jb + 1] = jnp.concatenate([jnp.where(row_b < half, 0.0, vt).astype(BF16), ones_rows[1]], axis=0)
    row_o = lax.broadcasted_iota(jnp.int32, (LANES, 2 * BLOCK), 0)
    neg_blk = jnp.full((BLOCK, BLOCK), -jnp.inf, F32)

    def start(j):
        return j * BLOCK if isinstance(j, int) else pl.multiple_of(j * BLOCK, BLOCK)

    def stage_scores(j, s_ref):
        qt = qt_ref[j]
        k0 = start(j)
        kzb = jnp.concatenate([kz_ref[0, pl.ds(k0, BAND), :], kz_ref[1, pl.ds(k0, BAND), :]], axis=0)
        for pair in range(2):
            rows = 2 * LANES * pair
            rhs = jnp.concatenate([qt[rows:rows + LANES, :], qt[rows + LANES:rows + 2 * LANES, :]], axis=1)
            s_ref[pair] = _dot(kzb, rhs)

    def stage_softmax(s_ref, p_ref, l_ref, edge):
        for pair in range(2):
            for kv in range(2):
                for t in range(2):
                    h = 2 * pair + t + 4 * kv
                    rs = slice(kv * BAND, (kv + 1) * BAND)
                    cs = slice(t * BLOCK, (t + 1) * BLOCK)
                    if edge < 0:
                        bias = jnp.concatenate([neg_blk, bias_ref[h, BLOCK:BAND, :]], axis=0)
                    elif edge > 0:
                        bias = jnp.concatenate([bias_ref[h, 0:2 * BLOCK, :], neg_blk], axis=0)
                    else:
                        bias = bias_ref[h]
                    s = s_ref[pair, rs, cs] + bias
                    sink = sink_ref[h] * LOG2_E
                    m = jnp.maximum(jnp.max(s, axis=0, keepdims=True), sink)
                    l_ref[pair, kv:kv + 1, cs] = jnp.exp2(sink - m)
                    p_ref[pair, rs, cs] = jnp.exp2(s - m).astype(BF16)

    def stage_pv(j, p_ref, l_ref):
        q0 = start(j)
        vtb = jnp.concatenate([vtz_ref[kv, j + i] for kv in range(2) for i in range(3)], axis=1)
        for pair in range(2):
            ot = _dot(vtb, p_ref[pair])
            inv = 1.0 / (ot[LANES:LANES + 2, :] + l_ref[pair, 0:2, :])
            ot = ot[:LANES, :] * jnp.where(row_o < half, inv[0:1, :], inv[1:2, :])
            for t in range(2):
                c0 = (2 * pair + t) * LANES
                out_ref[pl.ds(q0, BLOCK), c0:c0 + LANES] = ot[:, t * BLOCK:(t + 1) * BLOCK].T.astype(BF16)

    stage_scores(0, sa_ref)
    stage_scores(1, sb_ref)
    stage_softmax(sa_ref, pa_ref, la_ref, -1)

    def two_blocks(ii, carry):
        j = 2 * ii
        stage_pv(j - 2, pa_ref, la_ref)
        stage_scores(j, sa_ref)
        stage_softmax(sb_ref, pb_ref, lb_ref, 0)
        stage_pv(j - 1, pb_ref, lb_ref)
        stage_scores(j + 1, sb_ref)
        stage_softmax(sa_ref, pa_ref, la_ref, 0)
        return carry

    lax.fori_loop(1, N_QBLOCKS // 2, two_blocks, 0)
    stage_softmax(sb_ref, pb_ref, lb_ref, 1)
    stage_pv(N_QBLOCKS - 2, pa_ref, la_ref)
    stage_pv(N_QBLOCKS - 1, pb_ref, lb_ref)


def _attention(qt, proj, bias, sink, batch):
    t = proj.shape[0]
    kv_spec = lambda off: pl.BlockSpec((SEQ, ATT_KV_WIDTH), lambda b: (b, off // ATT_KV_WIDTH))
    return pl.pallas_call(
        _attn_kernel,
        grid=(batch,),
        in_specs=[
            pl.BlockSpec(memory_space=pltpu.SMEM),
            pl.BlockSpec((N_QBLOCKS, ATT_WIDTH, BLOCK), lambda b: (b, 0, 0)),
            kv_spec(OFF_KA),
            kv_spec(OFF_VA),
            pl.BlockSpec((ATT_HEADS, 3 * BLOCK, BLOCK), lambda b: (0, 0, 0)),
        ],
        out_specs=pl.BlockSpec((SEQ, ATT_WIDTH), lambda b: (b, 0)),
        out_shape=jax.ShapeDtypeStruct((t, ATT_WIDTH), BF16),
        scratch_shapes=[
            pltpu.VMEM((2, SEQ + 2 * BLOCK, ATT_KV_WIDTH), BF16),
            pltpu.VMEM((2, N_QBLOCKS + 2, ATT_AUG_ROWS, BLOCK), BF16),
            pltpu.VMEM((2, 2 * BAND, 2 * BLOCK), F32),
            pltpu.VMEM((2, 2 * BAND, 2 * BLOCK), F32),
            pltpu.VMEM((2, 2 * BAND, 2 * BLOCK), BF16),
            pltpu.VMEM((2, 2 * BAND, 2 * BLOCK), BF16),
            pltpu.VMEM((2, 8, 2 * BLOCK), F32),
            pltpu.VMEM((2, 8, 2 * BLOCK), F32),
        ],
        compiler_params=pltpu.CompilerParams(
            dimension_semantics=("arbitrary",), vmem_limit_bytes=VMEM_LIMIT),
        name="attention",
    )(sink, qt, proj, proj, bias)


def _log_sigmoid(x):
    return jnp.minimum(x, 0.0) - jnp.log1p(jnp.exp(-jnp.abs(x)))


def _split3(x):
    hi = x.astype(BF16)
    r1 = x - hi.astype(F32)
    mid = r1.astype(BF16)
    lo = (r1 - mid.astype(F32)).astype(BF16)
    return hi, mid, lo


ROWS_PER_CHAIN = 6
N_CHAIN_ROWS = 2 * ML_HEADS * ROWS_PER_CHAIN


def _mlstm_kernel(qt_ref, ks_ref, vt_ref, o_ref, gt_ref, gb_ref, ng_ref, out_ref,
                  hf_ref, hb_ref, cf_ref, cb_ref, rows_ref):
    L = ML_CHUNK
    d = ML_HEAD_DIM
    nh = ML_HEADS
    si = lax.broadcasted_iota(jnp.int32, (L, L), 0)
    ti = lax.broadcasted_iota(jnp.int32, (L, L), 1)
    masks = (si <= ti, si >= ti)
    ones_le = jnp.where(si <= ti, 1.0, 0.0).astype(BF16)
    ones_ge = jnp.where(si >= ti, 1.0, 0.0).astype(BF16)
    eye = jnp.where(si == ti, 1.0, 0.0).astype(BF16)

    lane16 = lax.broadcasted_iota(jnp.int32, (N_CHUNKS, L), 1)
    row16 = lax.broadcasted_iota(jnp.int32, (N_CHUNKS, L), 0)
    for hh in range(nh):
        for direction in range(2):
            fwd = direction == 0
            k_i, k_f = 2 * direction, 2 * direction + 1
            li = gt_ref[hh, :, k_i, :] + gb_ref[hh, k_i:k_i + 1, :]
            lf = _log_sigmoid(gt_ref[hh, :, k_f, :] + gb_ref[hh, k_f:k_f + 1, :])
            ones = ones_le if fwd else ones_ge
            b = sum(_dot(p, ones) for p in _split3(lf))
            b_last = jnp.broadcast_to(b[:, L - 1:L] if fwd else b[:, 0:1], (N_CHUNKS, L))
            a = b_last - b + li
            a_max = jnp.broadcast_to(jnp.max(a, axis=-1, keepdims=True), (N_CHUNKS, L))
            r = li - b
            cm = r
            for sh in (1, 2, 4, 8, 16, 32, 64):
                if fwd:
                    cm = jnp.maximum(cm, jnp.where(lane16 >= sh, pltpu.roll(cm, sh, 1), -jnp.inf))
                else:
                    cm = jnp.maximum(cm, jnp.where(lane16 < L - sh, pltpu.roll(cm, L - sh, 1), -jnp.inf))
            m_cur = jnp.zeros((1, L), F32)
            m_prev = jnp.zeros((N_CHUNKS, L), F32)
            m_new = jnp.zeros((N_CHUNKS, L), F32)
            for c in (range(N_CHUNKS) if fwd else reversed(range(N_CHUNKS))):
                m_prev = jnp.where(row16 == c, m_cur, m_prev)
                m_cur = jnp.maximum(b_last[c:c + 1, :] + m_cur, a_max[c:c + 1, :])
                m_new = jnp.where(row16 == c, m_cur, m_new)
            m_t = b + jnp.maximum(m_prev, cm)
            base = (2 * hh + direction) * ROWS_PER_CHAIN
            for j, val in enumerate((b - m_t, jnp.exp(b + m_prev - m_t), jnp.exp(-m_t),
                                     jnp.exp(a - m_new), r, jnp.exp(b_last + m_prev - m_new))):
                rows_ref[:, base + j, :] = val

    cf_ref[...] = jnp.zeros(cf_ref.shape, F32)
    cb_ref[...] = jnp.zeros(cb_ref.shape, F32)

    chains = [(hh, direction) for hh in range(nh) for direction in range(2)]
    c_refs = (cf_ref, cb_ref)
    h_refs = (hf_ref, hb_ref)

    def body(i, carry):
        chunks = (i, N_CHUNKS - 1 - i)
        tiles = (rows_ref[chunks[0]], rows_ref[chunks[1]])

        def row(n, j):
            hh, direction = chains[n]
            base = (2 * hh + direction) * ROWS_PER_CHAIN + j
            return tiles[direction][base:base + 1, :]

        r_rows = jnp.concatenate([row(n, 4) for n in range(len(chains))], axis=0)
        r_cols = sum(_dot_nt(eye, p) for p in _split3(r_rows))
        kcs, qtcs, vtas, kqs = [], [], [], []
        for hh, direction in chains:
            c = chunks[direction]
            r0 = pl.multiple_of(c * L, L)
            kcs.append(ks_ref[pl.ds(r0, L), hh * d:(hh + 1) * d])
            qtcs.append(qt_ref[hh, c])
            vtas.append(vt_ref[hh, c])
            kqs.append(_dot(kcs[-1], qtcs[-1]))
        rhss = []
        for n, (hh, direction) in enumerate(chains):
            st = kqs[n] * jnp.exp(jnp.where(masks[direction], r_cols[:, n:n + 1] + row(n, 0), -jnp.inf))
            wq = qtcs[n].astype(F32) * row(n, 1)
            rhss.append(jnp.concatenate([st.astype(BF16), wq.astype(BF16)], axis=0))
        states = [c_refs[direction][hh] for hh, direction in chains]
        nds = [_dot(jnp.concatenate([vtas[n], states[n].astype(BF16)], axis=1), rhss[n])
               for n in range(len(chains))]
        for n, (hh, direction) in enumerate(chains):
            nd = nds[n]
            h_refs[direction][hh, chunks[direction]] = (
                nd[:d, :] * (1.0 / jnp.maximum(jnp.abs(nd[d:d + 1, :]), row(n, 2))))
        for n, (hh, direction) in enumerate(chains):
            vw = (vtas[n].astype(F32) * row(n, 3)).astype(BF16)
            c_refs[direction][hh] = row(n, 5) * states[n] + _dot(vw, kcs[n])
        return carry

    lax.fori_loop(0, N_CHUNKS, body, 0)

    def finish(c, carry):
        r0 = pl.multiple_of(c * L, L)
        for hh in range(nh):
            sl = slice(hh * d, (hh + 1) * d)
            h = _rms((hf_ref[hh, c] + hb_ref[hh, c]).T, ng_ref[:, sl])
            o = o_ref[pl.ds(r0, L), sl].astype(F32)
            out_ref[pl.ds(r0, L), sl] = (h * (1.0 / (1.0 + jnp.exp(-o)))).astype(BF16)
        return carry

    lax.fori_loop(0, N_CHUNKS, finish, 0)


def _mlstm(qmt, proj, vta, gt, gbias, norm_g, batch):
    t = proj.shape[0]
    col = lambda off: pl.BlockSpec((SEQ, ML_WIDTH), lambda b: (b, off // ML_WIDTH))
    per_chunk = lambda rows: pl.BlockSpec((ML_HEADS, N_CHUNKS, rows, ML_CHUNK), lambda b: (0, b, 0, 0))
    return pl.pallas_call(
        _mlstm_kernel,
        grid=(batch,),
        in_specs=[
            per_chunk(ML_HEAD_DIM), col(0), per_chunk(AUG_ROWS), col(OFF_OM),
            pl.BlockSpec((ML_HEADS, N_CHUNKS, GATE_ROWS, LANES), lambda b: (0, b, 0, 0)),
            pl.BlockSpec((ML_HEADS, GATE_ROWS, 1), lambda b: (0, 0, 0)),
            pl.BlockSpec((1, ML_WIDTH), lambda b: (0, 0)),
        ],
        out_specs=pl.BlockSpec((SEQ, ML_WIDTH), lambda b: (b, 0)),
        out_shape=jax.ShapeDtypeStruct((t, ML_WIDTH), BF16),
        scratch_shapes=[
            pltpu.VMEM((ML_HEADS, N_CHUNKS, ML_HEAD_DIM, ML_CHUNK), F32),
            pltpu.VMEM((ML_HEADS, N_CHUNKS, ML_HEAD_DIM, ML_CHUNK), F32),
            pltpu.VMEM((ML_HEADS, AUG_ROWS, ML_HEAD_DIM), F32),
            pltpu.VMEM((ML_HEADS, AUG_ROWS, ML_HEAD_DIM), F32),
            pltpu.VMEM((N_CHUNKS, N_CHAIN_ROWS, LANES), F32),
        ],
        compiler_params=pltpu.CompilerParams(
            dimension_semantics=("arbitrary",), vmem_limit_bytes=VMEM_LIMIT),
        name="mlstm",
    )(qmt, proj, vta, proj, gt, gbias, norm_g)


def _ffn_kernel(att_ref, ml_ref, x_ref, wo_ref, g2_ref, wup_ref, wdn_ref, gf_ref, out_ref):
    x1 = (x_ref[...] + _dot(att_ref[...], wo_ref[:ATT_WIDTH, :])
          + _dot(ml_ref[...], wo_ref[ATT_WIDTH:, :]))
    hn = _rms(x1, g2_ref[...]).astype(BF16)
    out_ref[...] = x1
    for c0 in range(0, D_FF, FF_CHUNK):
        hid = _dot(hn, wup_ref[:, c0:c0 + FF_CHUNK])
        act = jnp.square(jnp.maximum(hid, 0.0)).astype(BF16)
        out_ref[...] += _dot(act, wdn_ref[c0:c0 + FF_CHUNK, :])
    out_ref[...] = _rms(out_ref[...], gf_ref[...])


def _ffn(att, ml, x2, w_out, g2, w_up, w_down, gf):
    t = x2.shape[0]
    row = lambda w: pl.BlockSpec((ROW_TILE, w), lambda i: (i, 0))
    whole = lambda a: pl.BlockSpec(a.shape, lambda i: (0, 0), pipeline_mode=pl.Buffered(1))
    return pl.pallas_call(
        _ffn_kernel,
        grid=(t // ROW_TILE,),
        in_specs=[row(ATT_WIDTH), row(ML_WIDTH), row(D_MODEL), whole(w_out), whole(g2),
                  whole(w_up), whole(w_down), whole(gf)],
        out_specs=row(D_MODEL),
        out_shape=jax.ShapeDtypeStruct((t, D_MODEL), F32),
        compiler_params=pltpu.CompilerParams(
            dimension_semantics=("arbitrary",), vmem_limit_bytes=VMEM_LIMIT),
        name="outproj_ffn",
    )(att, ml, x2, w_out, g2, w_up, w_down, gf)


def _t5_bucket(rel):
    nb = REL_BUCKETS // 2
    max_exact = nb // 2
    ret = jnp.where(rel > 0, nb, 0)
    n = jnp.abs(rel)
    nf = jnp.maximum(n, 1).astype(jnp.float32)
    large = max_exact + (jnp.log(nf / max_exact) / math.log(REL_MAX_DIST / max_exact)
                         * (nb - max_exact)).astype(jnp.int32)
    large = jnp.minimum(large, nb - 1)
    return ret + jnp.where(n < max_exact, n, large)


def kernel(x, norm1_g, w_in, b_gates, conv_w, ml_norm_g, sink_logits, w_out, norm2_g, w_up, w_down,
           rel_bias, final_g):
    batch, seq, d_model = x.shape
    assert (seq, d_model) == (SEQ, D_MODEL) and w_in.shape[0] == 1
    x2 = x.reshape(batch * seq, d_model)

    w = w_in[0]
    att_cols = ATT_WIDTH + 2 * ATT_KV_WIDTH
    gate_cols = w[:, att_cols + 4 * ML_WIDTH:].reshape(d_model, 4, ML_HEADS).transpose(0, 2, 1)
    gate_cols = jnp.pad(gate_cols, ((0, 0), (0, 0), (0, GATE_ROWS - 4))).reshape(d_model, -1)
    gate_cols = jnp.pad(gate_cols, ((0, 0), (0, LANES - gate_cols.shape[1])))
    w_pad = jnp.concatenate([w[:, att_cols:att_cols + 4 * ML_WIDTH], w[:, ATT_WIDTH:att_cols], gate_cols],
                            axis=1).astype(BF16)
    slots = jnp.asarray(ATT_SLOT_HEADS)
    wq_t = w[:, :ATT_WIDTH].reshape(d_model, ATT_HEADS, ATT_HEAD_DIM)[:, slots, :]
    wq_t = wq_t.reshape(d_model, ATT_WIDTH).T.astype(BF16)
    gbias = jnp.pad(b_gates[0].reshape(4, ML_HEADS).T, ((0, 0), (0, GATE_ROWS - 4)))[..., None]

    proj, qt, gt, qmt, vta = _inproj(x2, norm1_g, w_pad, wq_t, conv_w[0])

    c = jnp.arange(3 * BLOCK)[:, None]
    r = jnp.arange(BLOCK)[None, :]
    bucket_t = _t5_bucket(c - BLOCK - r).astype(jnp.int32)
    bias = _bias_table(rel_bias.astype(F32), bucket_t)
    att = _attention(qt, proj, bias, sink_logits[0].astype(F32), batch)

    ml = _mlstm(qmt, proj, vta, gt, gbias.astype(F32), ml_norm_g, batch)

    wo = w_out[0]
    wo_att = wo[:ATT_WIDTH].reshape(ATT_HEADS, ATT_HEAD_DIM, d_model)[slots].reshape(ATT_WIDTH, d_model)
    wo = jnp.concatenate([wo_att, wo[ATT_WIDTH:]], axis=0)
    out = _ffn(att, ml, x2, wo.astype(BF16), norm2_g, w_up[0].astype(BF16),
               w_down[0].astype(BF16), final_g.reshape(1, d_model))
    return out.reshape(batch, seq, d_model)
```

```python
import functools
import math

import jax
import jax.numpy as jnp
from jax import lax
from jax.experimental import pallas as pl
from jax.experimental.pallas import tpu as pltpu

D_MODEL = 1024
SEQ = 2048
ATT_HEADS = 8
ATT_KV_HEADS = 2
ATT_HEAD_DIM = 64
ATT_WIDTH = ATT_HEADS * ATT_HEAD_DIM
ATT_KV_WIDTH = ATT_KV_HEADS * ATT_HEAD_DIM
WINDOW = 128
BLOCK = 128
REL_BUCKETS = 32
REL_MAX_DIST = 128
ML_HEADS = 4
ML_HEAD_DIM = 128
ML_WIDTH = ML_HEADS * ML_HEAD_DIM
ML_CHUNK = 128
N_GATE_COLS = 4 * ML_HEADS
D_FF = 4 * D_MODEL
EPS = 1e-6

LANES = 128
N_CHUNKS = SEQ // ML_CHUNK
GATE_ROWS = 8
MAIN_WIDTH = 4 * ML_WIDTH + 2 * ATT_KV_WIDTH
PROJ_PAD = MAIN_WIDTH + LANES
OFF_OM = ML_WIDTH
OFF_KA = 2 * ML_WIDTH
OFF_VA = OFF_KA + ATT_KV_WIDTH
PROJ_OUT = OFF_VA + ATT_KV_WIDTH
HALO = 8
AUG_ROWS = ML_HEAD_DIM + 16

ROW_TILE = 512
FF_CHUNK = 1024
VMEM_LIMIT = 56 * 1024 * 1024

F32 = jnp.float32
BF16 = jnp.bfloat16
NT_DIMS = (((1,), (1,)), ((), ()))


def _dot(a, b):
    return jnp.dot(a, b, preferred_element_type=F32)


def _dot_nt(a, b):
    return lax.dot_general(a, b, NT_DIMS, preferred_element_type=F32)


def _rms(x, g):
    ms = jnp.mean(x * x, axis=-1, keepdims=True)
    return x * lax.rsqrt(ms + EPS) * g


def _inproj_kernel(x_ref, xp_ref, xn_ref, g_ref, w_ref, wqt_ref, cw_ref,
                   proj_ref, qt_ref, gt_ref, qmt_ref, vta_ref, aq_ref, ak_ref, av_ref):
    i = pl.program_id(0)
    tiles_per_seq = SEQ // ROW_TILE
    has_prev = (i % tiles_per_seq) != 0
    has_next = (i % tiles_per_seq) != tiles_per_seq - 1
    g = g_ref[...]
    u = _rms(x_ref[...], g).astype(BF16)
    u_halo = _rms(jnp.concatenate([xp_ref[...], xn_ref[...]], axis=0), g).astype(BF16)
    u_ext = jnp.concatenate([u, u_halo], axis=0)
    row = lax.broadcasted_iota(jnp.int32, (ROW_TILE, ML_WIDTH), 0)

    def conv_silu(acc, w):
        cur = acc[0:ROW_TILE, :]
        halo_p = jnp.where(has_prev, acc[ROW_TILE + HALO - 1:ROW_TILE + HALO, :], 0.0)
        halo_n = jnp.where(has_next, acc[ROW_TILE + HALO:ROW_TILE + HALO + 1, :], 0.0)
        prev = jnp.where(row == 0, halo_p, pltpu.roll(cur, 1, 0))
        nxt = jnp.where(row == ROW_TILE - 1, halo_n, pltpu.roll(cur, ROW_TILE - 1, 0))
        y = w[0:1, :] * prev + w[1:2, :] * cur + w[2:3, :] * nxt
        return y * (1.0 / (1.0 + jnp.exp(-y)))

    aug_tail = jnp.where(lax.broadcasted_iota(jnp.int32, (AUG_ROWS - ML_HEAD_DIM, ML_CHUNK), 0) == 0,
                         1.0, 0.0).astype(BF16)
    tiles = [(hh, cc) for hh in range(ML_HEADS) for cc in range(ROW_TILE // ML_CHUNK)]

    def tile_t(a, hh, cc):
        return a[cc * ML_CHUNK:(cc + 1) * ML_CHUNK, hh * ML_HEAD_DIM:(hh + 1) * ML_HEAD_DIM].T.astype(BF16)

    aq_ref[...] = _dot(u_ext, w_ref[:, 0:ML_WIDTH])
    ak_ref[...] = _dot(u_ext, w_ref[:, ML_WIDTH:2 * ML_WIDTH])
    q = conv_silu(aq_ref, cw_ref[:, 0:ML_WIDTH])
    for hh, cc in tiles:
        qmt_ref[hh, cc] = tile_t(q, hh, cc)
    av_ref[...] = _dot(u, w_ref[:, 2 * ML_WIDTH:3 * ML_WIDTH])
    k = conv_silu(ak_ref, cw_ref[:, ML_WIDTH:2 * ML_WIDTH])
    proj_ref[:, 0:ML_WIDTH] = (k * (ML_HEAD_DIM ** -0.5)).astype(BF16)
    proj_ref[:, ML_WIDTH:2 * ML_WIDTH] = _dot(u, w_ref[:, 3 * ML_WIDTH:4 * ML_WIDTH]).astype(BF16)
    for hh, cc in tiles:
        vta_ref[hh, cc] = jnp.concatenate([tile_t(av_ref, hh, cc), aug_tail], axis=0)
    proj_ref[:, 2 * ML_WIDTH:PROJ_OUT] = _dot(u, w_ref[:, 4 * ML_WIDTH:MAIN_WIDTH]).astype(BF16)
    qt = (_dot_nt(wqt_ref[...], u) * (ATT_HEAD_DIM ** -0.5 * LOG2_E)).astype(BF16)
    for cc in range(ROW_TILE // BLOCK):
        qt_ref[cc] = qt[:, cc * BLOCK:(cc + 1) * BLOCK]
    gates = _dot(u, w_ref[:, MAIN_WIDTH:PROJ_PAD])
    for cc in range(ROW_TILE // ML_CHUNK):
        gt = gates[cc * ML_CHUNK:(cc + 1) * ML_CHUNK, :].T
        for h in range(ML_HEADS):
            gt_ref[h, cc] = gt[h * GATE_ROWS:(h + 1) * GATE_ROWS, :]


def _inproj(x2, g1, w_pad, wq_t, conv_w):
    t = x2.shape[0]
    halo_blocks = ROW_TILE // HALO
    chunks = ROW_TILE // ML_CHUNK
    return pl.pallas_call(
        _inproj_kernel,
        grid=(t // ROW_TILE,),
        in_specs=[
            pl.BlockSpec((ROW_TILE, D_MODEL), lambda i: (i, 0)),
            pl.BlockSpec((HALO, D_MODEL), lambda i: (jnp.maximum(i * halo_blocks - 1, 0), 0)),
            pl.BlockSpec((HALO, D_MODEL), lambda i: (jnp.minimum((i + 1) * halo_blocks, t // HALO - 1), 0)),
            pl.BlockSpec((1, D_MODEL), lambda i: (0, 0)),
            pl.BlockSpec((D_MODEL, PROJ_PAD), lambda i: (0, 0), pipeline_mode=pl.Buffered(1)),
            pl.BlockSpec((ATT_WIDTH, D_MODEL), lambda i: (0, 0), pipeline_mode=pl.Buffered(1)),
            pl.BlockSpec((3, 2 * ML_WIDTH), lambda i: (0, 0)),
        ],
        out_specs=[
            pl.BlockSpec((ROW_TILE, PROJ_OUT), lambda i: (i, 0)),
            pl.BlockSpec((ROW_TILE // BLOCK, ATT_WIDTH, BLOCK), lambda i: (i, 0, 0)),
            pl.BlockSpec((ML_HEADS, chunks, GATE_ROWS, LANES), lambda i: (0, i, 0, 0)),
            pl.BlockSpec((ML_HEADS, chunks, ML_HEAD_DIM, ML_CHUNK), lambda i: (0, i, 0, 0)),
            pl.BlockSpec((ML_HEADS, chunks, AUG_ROWS, ML_CHUNK), lambda i: (0, i, 0, 0)),
        ],
        out_shape=[
            jax.ShapeDtypeStruct((t, PROJ_OUT), BF16),
            jax.ShapeDtypeStruct((t // BLOCK, ATT_WIDTH, BLOCK), BF16),
            jax.ShapeDtypeStruct((ML_HEADS, t // ML_CHUNK, GATE_ROWS, LANES), F32),
            jax.ShapeDtypeStruct((ML_HEADS, t // ML_CHUNK, ML_HEAD_DIM, ML_CHUNK), BF16),
            jax.ShapeDtypeStruct((ML_HEADS, t // ML_CHUNK, AUG_ROWS, ML_CHUNK), BF16),
        ],
        scratch_shapes=[
            pltpu.VMEM((ROW_TILE + 2 * HALO, ML_WIDTH), F32),
            pltpu.VMEM((ROW_TILE + 2 * HALO, ML_WIDTH), F32),
            pltpu.VMEM((ROW_TILE, ML_WIDTH), F32),
        ],
        compiler_params=pltpu.CompilerParams(
            dimension_semantics=("arbitrary",), vmem_limit_bytes=VMEM_LIMIT),
        name="inproj",
    )(x2, x2, x2, g1, w_pad, wq_t, conv_w)


def _bias_kernel(rb_ref, bucket_ref, out_ref):
    bucket = bucket_ref[...]
    c = lax.broadcasted_iota(jnp.int32, bucket.shape, 0)
    r = lax.broadcasted_iota(jnp.int32, bucket.shape, 1)
    valid = jnp.abs(c - BLOCK - r) <= WINDOW
    for h in range(ATT_HEADS):
        acc = jnp.zeros(bucket.shape, F32)
        for b in range(REL_BUCKETS):
            acc = jnp.where(bucket == b, rb_ref[b, h], acc)
        out_ref[h] = jnp.where(valid, acc * LOG2_E, -jnp.inf)


def _bias_table(rel_bias, bucket_t):
    return pl.pallas_call(
        _bias_kernel,
        in_specs=[
            pl.BlockSpec(memory_space=pltpu.SMEM),
            pl.BlockSpec(memory_space=pltpu.VMEM),
        ],
        out_specs=pl.BlockSpec(memory_space=pltpu.VMEM),
        out_shape=jax.ShapeDtypeStruct((ATT_HEADS, 3 * BLOCK, BLOCK), F32),
        name="bias_table",
    )(rel_bias, bucket_t)


N_QBLOCKS = SEQ // BLOCK
ATT_SLOT_HEADS = tuple(p // 2 + 4 * (p % 2) for p in range(ATT_HEADS))


BAND = 3 * BLOCK
ATT_AUG_ROWS = LANES + 16
LOG2_E = math.log2(math.e)


def _attn_kernel(sink_ref, qt_ref, k_ref, v_ref, bias_ref, out_ref,
                 kz_ref, vtz_ref, sa_ref, sb_ref, pa_ref, pb_ref, la_ref, lb_ref):
    half = ATT_HEAD_DIM
    kf = k_ref[...].astype(F32)
    lane_s = lax.broadcasted_iota(jnp.int32, kf.shape, 1)
    zero_blk = jnp.zeros((BLOCK, LANES), BF16)
    row_t = lax.broadcasted_iota(jnp.int32, (ATT_AUG_ROWS - LANES, BLOCK), 0)
    ones_rows = [jnp.where(row_t == kv, 1.0, 0.0).astype(BF16) for kv in range(2)]
    for kv in range(2):
        kz_ref[kv, 0:BLOCK, :] = zero_blk
        kz_ref[kv, SEQ + BLOCK:SEQ + 2 * BLOCK, :] = zero_blk
        vtz_ref[kv, 0] = jnp.concatenate([zero_blk, ones_rows[kv]], axis=0)
        vtz_ref[kv, N_QBLOCKS + 1] = jnp.concatenate([zero_blk, ones_rows[kv]], axis=0)
    kz_ref[0, BLOCK:SEQ + BLOCK, :] = jnp.where(lane_s < half, kf, 0.0).astype(BF16)
    kz_ref[1, BLOCK:SEQ + BLOCK, :] = jnp.where(lane_s < half, 0.0, kf).astype(BF16)
    row_b = lax.broadcasted_iota(jnp.int32, (LANES, BLOCK), 0)
    for jb in range(N_QBLOCKS):
        vt = v_ref[jb * BLOCK:(jb + 1) * BLOCK, :].astype(F32).T
        vtz_ref[0, jb + 1] = jnp.concatenate([jnp.where(row_b < half, vt, 0.0).astype(BF16), ones_rows[0]], axis=0)
        vtz_ref[1, jb + 1] = jnp.concatenate([jnp.where(row_b < half, 0.0, vt).astype(BF16), ones_rows[1]], axis=0)
    row_o = lax.broadcasted_iota(jnp.int32, (LANES, 2 * BLOCK), 0)
    neg_blk = jnp.full((BLOCK, BLOCK), -jnp.inf, F32)

    def start(j):
        return j * BLOCK if isinstance(j, int) else pl.multiple_of(j * BLOCK, BLOCK)

    def stage_scores(j, s_ref):
        qt = qt_ref[j]
        k0 = start(j)
        kzb = jnp.concatenate([kz_ref[0, pl.ds(k0, BAND), :], kz_ref[1, pl.ds(k0, BAND), :]], axis=0)
        for pair in range(2):
            rows = 2 * LANES * pair
            rhs = jnp.concatenate([qt[rows:rows + LANES, :], qt[rows + LANES:rows + 2 * LANES, :]], axis=1)
            s_ref[pair] = _dot(kzb, rhs)

    def stage_softmax(s_ref, p_ref, l_ref, edge):
        for pair in range(2):
            for kv in range(2):
                for t in range(2):
                    h = 2 * pair + t + 4 * kv
                    rs = slice(kv * BAND, (kv + 1) * BAND)
                    cs = slice(t * BLOCK, (t + 1) * BLOCK)
                    if edge < 0:
                        bias = jnp.concatenate([neg_blk, bias_ref[h, BLOCK:BAND, :]], axis=0)
                    elif edge > 0:
                        bias = jnp.concatenate([bias_ref[h, 0:2 * BLOCK, :], neg_blk], axis=0)
                    else:
                        bias = bias_ref[h]
                    s = s_ref[pair, rs, cs] + bias
                    sink = sink_ref[h] * LOG2_E
                    m = jnp.maximum(jnp.max(s, axis=0, keepdims=True), sink)
                    l_ref[pair, kv:kv + 1, cs] = jnp.exp2(sink - m)
                    p_ref[pair, rs, cs] = jnp.exp2(s - m).astype(BF16)

    def stage_pv(j, p_ref, l_ref):
        q0 = start(j)
        vtb = jnp.concatenate([vtz_ref[kv, j + i] for kv in range(2) for i in range(3)], axis=1)
        for pair in range(2):
            ot = _dot(vtb, p_ref[pair])
            inv = 1.0 / (ot[LANES:LANES + 2, :] + l_ref[pair, 0:2, :])
            ot = ot[:LANES, :] * jnp.where(row_o < half, inv[0:1, :], inv[1:2, :])
            for t in range(2):
                c0 = (2 * pair + t) * LANES
                out_ref[pl.ds(q0, BLOCK), c0:c0 + LANES] = ot[:, t * BLOCK:(t + 1) * BLOCK].T.astype(BF16)

    stage_scores(0, sa_ref)
    stage_scores(1, sb_ref)
    stage_softmax(sa_ref, pa_ref, la_ref, -1)

    def two_blocks(ii, carry):
        j = 2 * ii
        stage_pv(j - 2, pa_ref, la_ref)
        stage_scores(j, sa_ref)
        stage_softmax(sb_ref, pb_ref, lb_ref, 0)
        stage_pv(j - 1, pb_ref, lb_ref)
        stage_scores(j + 1, sb_ref)
        stage_softmax(sa_ref, pa_ref, la_ref, 0)
        return carry

    lax.fori_loop(1, N_QBLOCKS // 2, two_blocks, 0)
    stage_softmax(sb_ref, pb_ref, lb_ref, 1)
    stage_pv(N_QBLOCKS - 2, pa_ref, la_ref)
    stage_pv(N_QBLOCKS - 1, pb_ref, lb_ref)


def _attention(qt, proj, bias, sink, batch):
    t = proj.shape[0]
    kv_spec = lambda off: pl.BlockSpec((SEQ, ATT_KV_WIDTH), lambda b: (b, off // ATT_KV_WIDTH))
    return pl.pallas_call(
        _attn_kernel,
        grid=(batch,),
        in_specs=[
            pl.BlockSpec(memory_space=pltpu.SMEM),
            pl.BlockSpec((N_QBLOCKS, ATT_WIDTH, BLOCK), lambda b: (b, 0, 0)),
            kv_spec(OFF_KA),
            kv_spec(OFF_VA),
            pl.BlockSpec((ATT_HEADS, 3 * BLOCK, BLOCK), lambda b: (0, 0, 0)),
        ],
        out_specs=pl.BlockSpec((SEQ, ATT_WIDTH), lambda b: (b, 0)),
        out_shape=jax.ShapeDtypeStruct((t, ATT_WIDTH), BF16),
        scratch_shapes=[
            pltpu.VMEM((2, SEQ + 2 * BLOCK, ATT_KV_WIDTH), BF16),
            pltpu.VMEM((2, N_QBLOCKS + 2, ATT_AUG_ROWS, BLOCK), BF16),
            pltpu.VMEM((2, 2 * BAND, 2 * BLOCK), F32),
            pltpu.VMEM((2, 2 * BAND, 2 * BLOCK), F32),
            pltpu.VMEM((2, 2 * BAND, 2 * BLOCK), BF16),
            pltpu.VMEM((2, 2 * BAND, 2 * BLOCK), BF16),
            pltpu.VMEM((2, 8, 2 * BLOCK), F32),
            pltpu.VMEM((2, 8, 2 * BLOCK), F32),
        ],
        compiler_params=pltpu.CompilerParams(
            dimension_semantics=("arbitrary",), vmem_limit_bytes=VMEM_LIMIT),
        name="attention",
    )(sink, qt, proj, proj, bias)


def _log_sigmoid(x):
    return jnp.minimum(x, 0.0) - jnp.log1p(jnp.exp(-jnp.abs(x)))


def _split3(x):
    hi = x.astype(BF16)
    r1 = x - hi.astype(F32)
    mid = r1.astype(BF16)
    lo = (r1 - mid.astype(F32)).astype(BF16)
    return hi, mid, lo


ROWS_PER_CHAIN = 6
N_CHAIN_ROWS = 2 * ML_HEADS * ROWS_PER_CHAIN


def _mlstm_kernel(qt_ref, ks_ref, vt_ref, gt_ref, gb_ref, out_ref, cf_ref, cb_ref, rows_ref):
    L = ML_CHUNK
    d = ML_HEAD_DIM
    nh = ML_HEADS
    si = lax.broadcasted_iota(jnp.int32, (L, L), 0)
    ti = lax.broadcasted_iota(jnp.int32, (L, L), 1)
    masks = (si <= ti, si >= ti)
    ones_le = jnp.where(si <= ti, 1.0, 0.0).astype(BF16)
    ones_ge = jnp.where(si >= ti, 1.0, 0.0).astype(BF16)
    eye = jnp.where(si == ti, 1.0, 0.0).astype(BF16)

    lane16 = lax.broadcasted_iota(jnp.int32, (N_CHUNKS, L), 1)
    row16 = lax.broadcasted_iota(jnp.int32, (N_CHUNKS, L), 0)
    for hh in range(nh):
        for direction in range(2):
            fwd = direction == 0
            k_i, k_f = 2 * direction, 2 * direction + 1
            li = gt_ref[hh, :, k_i, :] + gb_ref[hh, k_i:k_i + 1, :]
            lf = _log_sigmoid(gt_ref[hh, :, k_f, :] + gb_ref[hh, k_f:k_f + 1, :])
            ones = ones_le if fwd else ones_ge
            b = sum(_dot(p, ones) for p in _split3(lf))
            b_last = jnp.broadcast_to(b[:, L - 1:L] if fwd else b[:, 0:1], (N_CHUNKS, L))
            a = b_last - b + li
            a_max = jnp.broadcast_to(jnp.max(a, axis=-1, keepdims=True), (N_CHUNKS, L))
            r = li - b
            cm = r
            for sh in (1, 2, 4, 8, 16, 32, 64):
                if fwd:
                    cm = jnp.maximum(cm, jnp.where(lane16 >= sh, pltpu.roll(cm, sh, 1), -jnp.inf))
                else:
                    cm = jnp.maximum(cm, jnp.where(lane16 < L - sh, pltpu.roll(cm, L - sh, 1), -jnp.inf))
            m_cur = jnp.zeros((1, L), F32)
            m_prev = jnp.zeros((N_CHUNKS, L), F32)
            m_new = jnp.zeros((N_CHUNKS, L), F32)
            for c in (range(N_CHUNKS) if fwd else reversed(range(N_CHUNKS))):
                m_prev = jnp.where(row16 == c, m_cur, m_prev)
                m_cur = jnp.maximum(b_last[c:c + 1, :] + m_cur, a_max[c:c + 1, :])
                m_new = jnp.where(row16 == c, m_cur, m_new)
            m_t = b + jnp.maximum(m_prev, cm)
            base = (2 * hh + direction) * ROWS_PER_CHAIN
            for j, val in enumerate((b - m_t, jnp.exp(b + m_prev - m_t), jnp.exp(-m_t),
                                     jnp.exp(a - m_new), r, jnp.exp(b_last + m_prev - m_new))):
                rows_ref[:, base + j, :] = val

    cf_ref[...] = jnp.zeros(cf_ref.shape, F32)
    cb_ref[...] = jnp.zeros(cb_ref.shape, F32)

    chains = [(hh, direction) for hh in range(nh) for direction in range(2)]
    c_refs = (cf_ref, cb_ref)

    def body(i, carry, accumulate):
        chunks = (i, N_CHUNKS - 1 - i)
        tiles = (rows_ref[chunks[0]], rows_ref[chunks[1]])

        def row(n, j):
            hh, direction = chains[n]
            base = (2 * hh + direction) * ROWS_PER_CHAIN + j
            return tiles[direction][base:base + 1, :]

        r_rows = jnp.concatenate([row(n, 4) for n in range(len(chains))], axis=0)
        r_cols = sum(_dot_nt(eye, p) for p in _split3(r_rows))
        kcs, qtcs, vtas, kqs = [], [], [], []
        for hh, direction in chains:
            c = chunks[direction]
            r0 = pl.multiple_of(c * L, L)
            kcs.append(ks_ref[pl.ds(r0, L), hh * d:(hh + 1) * d])
            qtcs.append(qt_ref[hh, c])
            vtas.append(vt_ref[hh, c])
            kqs.append(_dot(kcs[-1], qtcs[-1]))
        rhss = []
        for n, (hh, direction) in enumerate(chains):
            st = kqs[n] * jnp.exp(jnp.where(masks[direction], r_cols[:, n:n + 1] + row(n, 0), -jnp.inf))
            wq = qtcs[n].astype(F32) * row(n, 1)
            rhss.append(jnp.concatenate([st.astype(BF16), wq.astype(BF16)], axis=0))
        states = [c_refs[direction][hh] for hh, direction in chains]
        nds = [_dot(jnp.concatenate([vtas[n], states[n].astype(BF16)], axis=1), rhss[n])
               for n in range(len(chains))]
        for n, (hh, direction) in enumerate(chains):
            nd = nds[n]
            h = nd[:d, :] * (1.0 / jnp.maximum(jnp.abs(nd[d:d + 1, :]), row(n, 2)))
            if accumulate:
                out_ref[hh, chunks[direction]] += h
            else:
                out_ref[hh, chunks[direction]] = h
        for n, (hh, direction) in enumerate(chains):
            vw = (vtas[n].astype(F32) * row(n, 3)).astype(BF16)
            c_refs[direction][hh] = row(n, 5) * states[n] + _dot(vw, kcs[n])
        return carry

    lax.fori_loop(0, N_CHUNKS // 2, functools.partial(body, accumulate=False), 0)
    lax.fori_loop(N_CHUNKS // 2, N_CHUNKS, functools.partial(body, accumulate=True), 0)


def _mlstm(qmt, proj, vta, gt, gbias, batch):
    t = proj.shape[0]
    per_chunk = lambda rows: pl.BlockSpec((ML_HEADS, N_CHUNKS, rows, ML_CHUNK), lambda b: (0, b, 0, 0))
    return pl.pallas_call(
        _mlstm_kernel,
        grid=(batch,),
        in_specs=[
            per_chunk(ML_HEAD_DIM),
            pl.BlockSpec((SEQ, ML_WIDTH), lambda b: (b, 0)),
            per_chunk(AUG_ROWS),
            pl.BlockSpec((ML_HEADS, N_CHUNKS, GATE_ROWS, LANES), lambda b: (0, b, 0, 0)),
            pl.BlockSpec((ML_HEADS, GATE_ROWS, 1), lambda b: (0, 0, 0)),
        ],
        out_specs=per_chunk(ML_HEAD_DIM),
        out_shape=jax.ShapeDtypeStruct((ML_HEADS, t // ML_CHUNK, ML_HEAD_DIM, ML_CHUNK), F32),
        scratch_shapes=[
            pltpu.VMEM((ML_HEADS, AUG_ROWS, ML_HEAD_DIM), F32),
            pltpu.VMEM((ML_HEADS, AUG_ROWS, ML_HEAD_DIM), F32),
            pltpu.VMEM((N_CHUNKS, N_CHAIN_ROWS, LANES), F32),
        ],
        compiler_params=pltpu.CompilerParams(
            dimension_semantics=("arbitrary",), vmem_limit_bytes=VMEM_LIMIT),
        name="mlstm",
    )(qmt, proj, vta, gt, gbias)


def _ffn_kernel(att_ref, ht_ref, o_ref, ng_ref, x_ref, wo_ref, g2_ref, wup_ref, wdn_ref, gf_ref, out_ref,
                ml_ref):
    for hh in range(ML_HEADS):
        cols = slice(hh * ML_HEAD_DIM, (hh + 1) * ML_HEAD_DIM)
        for cc in range(ROW_TILE // ML_CHUNK):
            rows = slice(cc * ML_CHUNK, (cc + 1) * ML_CHUNK)
            h = _rms(ht_ref[hh, cc].T, ng_ref[:, cols])
            gate = o_ref[rows, cols].astype(F32)
            ml_ref[rows, cols] = (h * (1.0 / (1.0 + jnp.exp(-gate)))).astype(BF16)
    x1 = (x_ref[...] + _dot(att_ref[...], wo_ref[:ATT_WIDTH, :])
          + _dot(ml_ref[...], wo_ref[ATT_WIDTH:, :]))
    hn = _rms(x1, g2_ref[...]).astype(BF16)
    out_ref[...] = x1
    for c0 in range(0, D_FF, FF_CHUNK):
        hid = _dot(hn, wup_ref[:, c0:c0 + FF_CHUNK])
        act = jnp.square(jnp.maximum(hid, 0.0)).astype(BF16)
        out_ref[...] += _dot(act, wdn_ref[c0:c0 + FF_CHUNK, :])
    out_ref[...] = _rms(out_ref[...], gf_ref[...])


def _ffn(att, ht, proj, ng, x2, w_out, g2, w_up, w_down, gf):
    t = x2.shape[0]
    row = lambda w: pl.BlockSpec((ROW_TILE, w), lambda i: (i, 0))
    whole = lambda a: pl.BlockSpec(a.shape, lambda i: (0, 0), pipeline_mode=pl.Buffered(1))
    chunks = ROW_TILE // ML_CHUNK
    return pl.pallas_call(
        _ffn_kernel,
        grid=(t // ROW_TILE,),
        in_specs=[row(ATT_WIDTH),
                  pl.BlockSpec((ML_HEADS, chunks, ML_HEAD_DIM, ML_CHUNK), lambda i: (0, i, 0, 0)),
                  pl.BlockSpec((ROW_TILE, ML_WIDTH), lambda i: (i, OFF_OM // ML_WIDTH)),
                  whole(ng), row(D_MODEL), whole(w_out), whole(g2), whole(w_up), whole(w_down), whole(gf)],
        out_specs=row(D_MODEL),
        out_shape=jax.ShapeDtypeStruct((t, D_MODEL), F32),
        scratch_shapes=[pltpu.VMEM((ROW_TILE, ML_WIDTH), BF16)],
        compiler_params=pltpu.CompilerParams(
            dimension_semantics=("arbitrary",), vmem_limit_bytes=VMEM_LIMIT),
        name="outproj_ffn",
    )(att, ht, proj, ng, x2, w_out, g2, w_up, w_down, gf)


def _t5_bucket(rel):
    nb = REL_BUCKETS // 2
    max_exact = nb // 2
    ret = jnp.where(rel > 0, nb, 0)
    n = jnp.abs(rel)
    nf = jnp.maximum(n, 1).astype(jnp.float32)
    large = max_exact + (jnp.log(nf / max_exact) / math.log(REL_MAX_DIST / max_exact)
                         * (nb - max_exact)).astype(jnp.int32)
    large = jnp.minimum(large, nb - 1)
    return ret + jnp.where(n < max_exact, n, large)


def kernel(x, norm1_g, w_in, b_gates, conv_w, ml_norm_g, sink_logits, w_out, norm2_g, w_up, w_down,
           rel_bias, final_g):
    batch, seq, d_model = x.shape
    assert (seq, d_model) == (SEQ, D_MODEL) and w_in.shape[0] == 1
    x2 = x.reshape(batch * seq, d_model)

    w = w_in[0]
    att_cols = ATT_WIDTH + 2 * ATT_KV_WIDTH
    gate_cols = w[:, att_cols + 4 * ML_WIDTH:].reshape(d_model, 4, ML_HEADS).transpose(0, 2, 1)
    gate_cols = jnp.pad(gate_cols, ((0, 0), (0, 0), (0, GATE_ROWS - 4))).reshape(d_model, -1)
    gate_cols = jnp.pad(gate_cols, ((0, 0), (0, LANES - gate_cols.shape[1])))
    w_pad = jnp.concatenate([w[:, att_cols:att_cols + 4 * ML_WIDTH], w[:, ATT_WIDTH:att_cols], gate_cols],
                            axis=1).astype(BF16)
    slots = jnp.asarray(ATT_SLOT_HEADS)
    wq_t = w[:, :ATT_WIDTH].reshape(d_model, ATT_HEADS, ATT_HEAD_DIM)[:, slots, :]
    wq_t = wq_t.reshape(d_model, ATT_WIDTH).T.astype(BF16)
    gbias = jnp.pad(b_gates[0].reshape(4, ML_HEADS).T, ((0, 0), (0, GATE_ROWS - 4)))[..., None]

    proj, qt, gt, qmt, vta = _inproj(x2, norm1_g, w_pad, wq_t, conv_w[0])

    c = jnp.arange(3 * BLOCK)[:, None]
    r = jnp.arange(BLOCK)[None, :]
    bucket_t = _t5_bucket(c - BLOCK - r).astype(jnp.int32)
    bias = _bias_table(rel_bias.astype(F32), bucket_t)
    att = _attention(qt, proj, bias, sink_logits[0].astype(F32), batch)

    ht = _mlstm(qmt, proj, vta, gt, gbias.astype(F32), batch)

    wo = w_out[0]
    wo_att = wo[:ATT_WIDTH].reshape(ATT_HEADS, ATT_HEAD_DIM, d_model)[slots].reshape(ATT_WIDTH, d_model)
    wo = jnp.concatenate([wo_att, wo[ATT_WIDTH:]], axis=0)
    out = _ffn(att, ht, proj, ml_norm_g, x2, wo.astype(BF16), norm2_g, w_up[0].astype(BF16),
               w_down[0].astype(BF16), final_g.reshape(1, d_model))
    return out.reshape(batch, seq, d_model)
```

```python
import functools
import math

import jax
import jax.numpy as jnp
from jax import lax
from jax.experimental import pallas as pl
from jax.experimental.pallas import tpu as pltpu

D_MODEL = 1024
SEQ = 2048
ATT_HEADS = 8
ATT_KV_HEADS = 2
ATT_HEAD_DIM = 64
ATT_WIDTH = ATT_HEADS * ATT_HEAD_DIM
ATT_KV_WIDTH = ATT_KV_HEADS * ATT_HEAD_DIM
WINDOW = 128
BLOCK = 128
REL_BUCKETS = 32
REL_MAX_DIST = 128
ML_HEADS = 4
ML_HEAD_DIM = 128
ML_WIDTH = ML_HEADS * ML_HEAD_DIM
ML_CHUNK = 128
N_GATE_COLS = 4 * ML_HEADS
D_FF = 4 * D_MODEL
EPS = 1e-6

LANES = 128
N_CHUNKS = SEQ // ML_CHUNK
GATE_ROWS = 8
MAIN_WIDTH = 4 * ML_WIDTH + 2 * ATT_KV_WIDTH
PROJ_PAD = MAIN_WIDTH + LANES
OFF_OM = ML_WIDTH
OFF_KA = 2 * ML_WIDTH
OFF_VA = OFF_KA + ATT_KV_WIDTH
PROJ_OUT = OFF_VA + ATT_KV_WIDTH
HALO = 8
AUG_ROWS = ML_HEAD_DIM + 16

ROW_TILE = 512
FF_CHUNK = 1024
VMEM_LIMIT = 56 * 1024 * 1024

F32 = jnp.float32
BF16 = jnp.bfloat16
NT_DIMS = (((1,), (1,)), ((), ()))


def _dot(a, b):
    return jnp.dot(a, b, preferred_element_type=F32)


def _dot_nt(a, b):
    return lax.dot_general(a, b, NT_DIMS, preferred_element_type=F32)


def _rms(x, g):
    ms = jnp.mean(x * x, axis=-1, keepdims=True)
    return x * lax.rsqrt(ms + EPS) * g


def _inproj_kernel(x_ref, xp_ref, xn_ref, g_ref, w_ref, wqt_ref, cw_ref,
                   proj_ref, qt_ref, gt_ref, qmt_ref, vta_ref, aq_ref, ak_ref, av_ref):
    i = pl.program_id(0)
    tiles_per_seq = SEQ // ROW_TILE
    has_prev = (i % tiles_per_seq) != 0
    has_next = (i % tiles_per_seq) != tiles_per_seq - 1
    g = g_ref[...]
    u = _rms(x_ref[...], g).astype(BF16)
    u_halo = _rms(jnp.concatenate([xp_ref[...], xn_ref[...]], axis=0), g).astype(BF16)
    u_ext = jnp.concatenate([u, u_halo], axis=0)
    row = lax.broadcasted_iota(jnp.int32, (ROW_TILE, ML_WIDTH), 0)

    def conv_silu(acc, w):
        cur = acc[0:ROW_TILE, :]
        halo_p = jnp.where(has_prev, acc[ROW_TILE + HALO - 1:ROW_TILE + HALO, :], 0.0)
        halo_n = jnp.where(has_next, acc[ROW_TILE + HALO:ROW_TILE + HALO + 1, :], 0.0)
        prev = jnp.where(row == 0, halo_p, pltpu.roll(cur, 1, 0))
        nxt = jnp.where(row == ROW_TILE - 1, halo_n, pltpu.roll(cur, ROW_TILE - 1, 0))
        y = w[0:1, :] * prev + w[1:2, :] * cur + w[2:3, :] * nxt
        return y * (1.0 / (1.0 + jnp.exp(-y)))

    aug_tail = jnp.where(lax.broadcasted_iota(jnp.int32, (AUG_ROWS - ML_HEAD_DIM, ML_CHUNK), 0) == 0,
                         1.0, 0.0).astype(BF16)
    tiles = [(hh, cc) for hh in range(ML_HEADS) for cc in range(ROW_TILE // ML_CHUNK)]

    def tile_t(a, hh, cc):
        return a[cc * ML_CHUNK:(cc + 1) * ML_CHUNK, hh * ML_HEAD_DIM:(hh + 1) * ML_HEAD_DIM].T.astype(BF16)

    aq_ref[...] = _dot(u_ext, w_ref[:, 0:ML_WIDTH])
    ak_ref[...] = _dot(u_ext, w_ref[:, ML_WIDTH:2 * ML_WIDTH])
    q = conv_silu(aq_ref, cw_ref[:, 0:ML_WIDTH])
    for hh, cc in tiles:
        qmt_ref[hh, cc] = tile_t(q, hh, cc)
    av_ref[...] = _dot(u, w_ref[:, 2 * ML_WIDTH:3 * ML_WIDTH])
    k = conv_silu(ak_ref, cw_ref[:, ML_WIDTH:2 * ML_WIDTH])
    proj_ref[:, 0:ML_WIDTH] = (k * (ML_HEAD_DIM ** -0.5)).astype(BF16)
    proj_ref[:, ML_WIDTH:2 * ML_WIDTH] = _dot(u, w_ref[:, 3 * ML_WIDTH:4 * ML_WIDTH]).astype(BF16)
    for hh, cc in tiles:
        vta_ref[hh, cc] = jnp.concatenate([tile_t(av_ref, hh, cc), aug_tail], axis=0)
    proj_ref[:, 2 * ML_WIDTH:PROJ_OUT] = _dot(u, w_ref[:, 4 * ML_WIDTH:MAIN_WIDTH]).astype(BF16)
    qt = (_dot_nt(wqt_ref[...], u) * (ATT_HEAD_DIM ** -0.5 * LOG2_E)).astype(BF16)
    for cc in range(ROW_TILE // BLOCK):
        qt_ref[cc] = qt[:, cc * BLOCK:(cc + 1) * BLOCK]
    gates = _dot(u, w_ref[:, MAIN_WIDTH:PROJ_PAD])
    for cc in range(ROW_TILE // ML_CHUNK):
        gt = gates[cc * ML_CHUNK:(cc + 1) * ML_CHUNK, :].T
        for h in range(ML_HEADS):
            gt_ref[h, cc] = gt[h * GATE_ROWS:(h + 1) * GATE_ROWS, :]


def _inproj(x2, g1, w_pad, wq_t, conv_w):
    t = x2.shape[0]
    halo_blocks = ROW_TILE // HALO
    chunks = ROW_TILE // ML_CHUNK
    return pl.pallas_call(
        _inproj_kernel,
        grid=(t // ROW_TILE,),
        in_specs=[
            pl.BlockSpec((ROW_TILE, D_MODEL), lambda i: (i, 0)),
            pl.BlockSpec((HALO, D_MODEL), lambda i: (jnp.maximum(i * halo_blocks - 1, 0), 0)),
            pl.BlockSpec((HALO, D_MODEL), lambda i: (jnp.minimum((i + 1) * halo_blocks, t // HALO - 1), 0)),
            pl.BlockSpec((1, D_MODEL), lambda i: (0, 0)),
            pl.BlockSpec((D_MODEL, PROJ_PAD), lambda i: (0, 0), pipeline_mode=pl.Buffered(1)),
            pl.BlockSpec((ATT_WIDTH, D_MODEL), lambda i: (0, 0), pipeline_mode=pl.Buffered(1)),
            pl.BlockSpec((3, 2 * ML_WIDTH), lambda i: (0, 0)),
        ],
        out_specs=[
            pl.BlockSpec((ROW_TILE, PROJ_OUT), lambda i: (i, 0)),
            pl.BlockSpec((ROW_TILE // BLOCK, ATT_WIDTH, BLOCK), lambda i: (i, 0, 0)),
            pl.BlockSpec((ML_HEADS, chunks, GATE_ROWS, LANES), lambda i: (0, i, 0, 0)),
            pl.BlockSpec((ML_HEADS, chunks, ML_HEAD_DIM, ML_CHUNK), lambda i: (0, i, 0, 0)),
            pl.BlockSpec((ML_HEADS, chunks, AUG_ROWS, ML_CHUNK), lambda i: (0, i, 0, 0)),
        ],
        out_shape=[
            jax.ShapeDtypeStruct((t, PROJ_OUT), BF16),
            jax.ShapeDtypeStruct((t // BLOCK, ATT_WIDTH, BLOCK), BF16),
            jax.ShapeDtypeStruct((ML_HEADS, t // ML_CHUNK, GATE_ROWS, LANES), F32),
            jax.ShapeDtypeStruct((ML_HEADS, t // ML_CHUNK, ML_HEAD_DIM, ML_CHUNK), BF16),
            jax.ShapeDtypeStruct((ML_HEADS, t // ML_CHUNK, AUG_ROWS, ML_CHUNK), BF16),
        ],
        scratch_shapes=[
            pltpu.VMEM((ROW_TILE + 2 * HALO, ML_WIDTH), F32),
            pltpu.VMEM((ROW_TILE + 2 * HALO, ML_WIDTH), F32),
            pltpu.VMEM((ROW_TILE, ML_WIDTH), F32),
        ],
        compiler_params=pltpu.CompilerParams(
            dimension_semantics=("arbitrary",), vmem_limit_bytes=VMEM_LIMIT),
        name="inproj",
    )(x2, x2, x2, g1, w_pad, wq_t, conv_w)


def _bias_kernel(rb_ref, bucket_ref, out_ref):
    bucket = bucket_ref[...]
    c = lax.broadcasted_iota(jnp.int32, bucket.shape, 0)
    r = lax.broadcasted_iota(jnp.int32, bucket.shape, 1)
    valid = jnp.abs(c - BLOCK - r) <= WINDOW
    for h in range(ATT_HEADS):
        acc = jnp.zeros(bucket.shape, F32)
        for b in range(REL_BUCKETS):
            acc = jnp.where(bucket == b, rb_ref[b, h], acc)
        out_ref[h] = jnp.where(valid, acc * LOG2_E, -jnp.inf)


def _bias_table(rel_bias, bucket_t):
    return pl.pallas_call(
        _bias_kernel,
        in_specs=[
            pl.BlockSpec(memory_space=pltpu.SMEM),
            pl.BlockSpec(memory_space=pltpu.VMEM),
        ],
        out_specs=pl.BlockSpec(memory_space=pltpu.VMEM),
        out_shape=jax.ShapeDtypeStruct((ATT_HEADS, 3 * BLOCK, BLOCK), F32),
        name="bias_table",
    )(rel_bias, bucket_t)


N_QBLOCKS = SEQ // BLOCK
ATT_SLOT_HEADS = tuple(p // 2 + 4 * (p % 2) for p in range(ATT_HEADS))


BAND = 3 * BLOCK
ATT_AUG_ROWS = LANES + 16
LOG2_E = math.log2(math.e)


def _attn_kernel(sink_ref, qt_ref, k_ref, v_ref, bias_ref, out_ref,
                 kz_ref, vtz_ref, sa_ref, sb_ref, pa_ref, pb_ref, la_ref, lb_ref):
    half = ATT_HEAD_DIM
    kf = k_ref[...].astype(F32)
    lane_s = lax.broadcasted_iota(jnp.int32, kf.shape, 1)
    zero_blk = jnp.zeros((BLOCK, LANES), BF16)
    row_t = lax.broadcasted_iota(jnp.int32, (ATT_AUG_ROWS - LANES, BLOCK), 0)
    ones_rows = [jnp.where(row_t == kv, 1.0, 0.0).astype(BF16) for kv in range(2)]
    for kv in range(2):
        kz_ref[kv, 0:BLOCK, :] = zero_blk
        kz_ref[kv, SEQ + BLOCK:SEQ + 2 * BLOCK, :] = zero_blk
        vtz_ref[kv, 0] = jnp.concatenate([zero_blk, ones_rows[kv]], axis=0)
        vtz_ref[kv, N_QBLOCKS + 1] = jnp.concatenate([zero_blk, ones_rows[kv]], axis=0)
    kz_ref[0, BLOCK:SEQ + BLOCK, :] = jnp.where(lane_s < half, kf, 0.0).astype(BF16)
    kz_ref[1, BLOCK:SEQ + BLOCK, :] = jnp.where(lane_s < half, 0.0, kf).astype(BF16)
    row_b = lax.broadcasted_iota(jnp.int32, (LANES, BLOCK), 0)
    for jb in range(N_QBLOCKS):
        vt = v_ref[jb * BLOCK:(jb + 1) * BLOCK, :].astype(F32).T
        vtz_ref[0, jb + 1] = jnp.concatenate([jnp.where(row_b < half, vt, 0.0).astype(BF16), ones_rows[0]], axis=0)
        vtz_ref[1, jb + 1] = jnp.concatenate([jnp.where(row_b < half, 0.0, vt).astype(BF16), ones_rows[1]], axis=0)
    row_o = lax.broadcasted_iota(jnp.int32, (LANES, 2 * BLOCK), 0)
    neg_blk = jnp.full((BLOCK, BLOCK), -jnp.inf, F32)

    def start(j):
        return j * BLOCK if isinstance(j, int) else pl.multiple_of(j * BLOCK, BLOCK)

    def stage_scores(j, s_ref):
        qt = qt_ref[j]
        k0 = start(j)
        kzb = jnp.concatenate([kz_ref[0, pl.ds(k0, BAND), :], kz_ref[1, pl.ds(k0, BAND), :]], axis=0)
        for pair in range(2):
            rows = 2 * LANES * pair
            rhs = jnp.concatenate([qt[rows:rows + LANES, :], qt[rows + LANES:rows + 2 * LANES, :]], axis=1)
            s_ref[pair] = _dot(kzb, rhs)

    def stage_softmax(s_ref, p_ref, l_ref, edge):
        for pair in range(2):
            for kv in range(2):
                for t in range(2):
                    h = 2 * pair + t + 4 * kv
                    rs = slice(kv * BAND, (kv + 1) * BAND)
                    cs = slice(t * BLOCK, (t + 1) * BLOCK)
                    if edge < 0:
                        bias = jnp.concatenate([neg_blk, bias_ref[h, BLOCK:BAND, :]], axis=0)
                    elif edge > 0:
                        bias = jnp.concatenate([bias_ref[h, 0:2 * BLOCK, :], neg_blk], axis=0)
                    else:
                        bias = bias_ref[h]
                    s = s_ref[pair, rs, cs] + bias
                    sink = sink_ref[h] * LOG2_E
                    m = jnp.maximum(jnp.max(s, axis=0, keepdims=True), sink)
                    l_ref[pair, kv:kv + 1, cs] = jnp.exp2(sink - m)
                    p_ref[pair, rs, cs] = jnp.exp2(s - m).astype(BF16)

    def stage_pv(j, p_ref, l_ref):
        q0 = start(j)
        vtb = jnp.concatenate([vtz_ref[kv, j + i] for kv in range(2) for i in range(3)], axis=1)
        for pair in range(2):
            ot = _dot(vtb, p_ref[pair])
            inv = 1.0 / (ot[LANES:LANES + 2, :] + l_ref[pair, 0:2, :])
            ot = ot[:LANES, :] * jnp.where(row_o < half, inv[0:1, :], inv[1:2, :])
            for t in range(2):
                c0 = (2 * pair + t) * LANES
                out_ref[pl.ds(q0, BLOCK), c0:c0 + LANES] = ot[:, t * BLOCK:(t + 1) * BLOCK].T.astype(BF16)

    stage_scores(0, sa_ref)
    stage_scores(1, sb_ref)
    stage_softmax(sa_ref, pa_ref, la_ref, -1)

    def two_blocks(ii, carry):
        j = 2 * ii
        stage_pv(j - 2, pa_ref, la_ref)
        stage_scores(j, sa_ref)
        stage_softmax(sb_ref, pb_ref, lb_ref, 0)
        stage_pv(j - 1, pb_ref, lb_ref)
        stage_scores(j + 1, sb_ref)
        stage_softmax(sa_ref, pa_ref, la_ref, 0)
        return carry

    lax.fori_loop(1, N_QBLOCKS // 2, two_blocks, 0)
    stage_softmax(sb_ref, pb_ref, lb_ref, 1)
    stage_pv(N_QBLOCKS - 2, pa_ref, la_ref)
    stage_pv(N_QBLOCKS - 1, pb_ref, lb_ref)


def _attention(qt, proj, bias, sink, batch):
    t = proj.shape[0]
    kv_spec = lambda off: pl.BlockSpec((SEQ, ATT_KV_WIDTH), lambda b: (b, off // ATT_KV_WIDTH))
    return pl.pallas_call(
        _attn_kernel,
        grid=(batch,),
        in_specs=[
            pl.BlockSpec(memory_space=pltpu.SMEM),
            pl.BlockSpec((N_QBLOCKS, ATT_WIDTH, BLOCK), lambda b: (b, 0, 0)),
            kv_spec(OFF_KA),
            kv_spec(OFF_VA),
            pl.BlockSpec((ATT_HEADS, 3 * BLOCK, BLOCK), lambda b: (0, 0, 0)),
        ],
        out_specs=pl.BlockSpec((SEQ, ATT_WIDTH), lambda b: (b, 0)),
        out_shape=jax.ShapeDtypeStruct((t, ATT_WIDTH), BF16),
        scratch_shapes=[
            pltpu.VMEM((2, SEQ + 2 * BLOCK, ATT_KV_WIDTH), BF16),
            pltpu.VMEM((2, N_QBLOCKS + 2, ATT_AUG_ROWS, BLOCK), BF16),
            pltpu.VMEM((2, 2 * BAND, 2 * BLOCK), F32),
            pltpu.VMEM((2, 2 * BAND, 2 * BLOCK), F32),
            pltpu.VMEM((2, 2 * BAND, 2 * BLOCK), BF16),
            pltpu.VMEM((2, 2 * BAND, 2 * BLOCK), BF16),
            pltpu.VMEM((2, 8, 2 * BLOCK), F32),
            pltpu.VMEM((2, 8, 2 * BLOCK), F32),
        ],
        compiler_params=pltpu.CompilerParams(
            dimension_semantics=("arbitrary",), vmem_limit_bytes=VMEM_LIMIT),
        name="attention",
    )(sink, qt, proj, proj, bias)


def _log_sigmoid(x):
    return jnp.minimum(x, 0.0) - jnp.log1p(jnp.exp(-jnp.abs(x)))


def _split3(x):
    hi = x.astype(BF16)
    r1 = x - hi.astype(F32)
    mid = r1.astype(BF16)
    lo = (r1 - mid.astype(F32)).astype(BF16)
    return hi, mid, lo


ROWS_PER_CHAIN = 6
N_CHAIN_ROWS = 2 * ML_HEADS * ROWS_PER_CHAIN


def _mlstm_kernel(qt_ref, ks_ref, vt_ref, gt_ref, gb_ref, out_ref, cf_ref, cb_ref, rows_ref, rhs_ref):
    L = ML_CHUNK
    d = ML_HEAD_DIM
    nh = ML_HEADS
    si = lax.broadcasted_iota(jnp.int32, (L, L), 0)
    ti = lax.broadcasted_iota(jnp.int32, (L, L), 1)
    masks = (si <= ti, si >= ti)
    ones_le = jnp.where(si <= ti, 1.0, 0.0).astype(BF16)
    ones_ge = jnp.where(si >= ti, 1.0, 0.0).astype(BF16)
    eye = jnp.where(si == ti, 1.0, 0.0).astype(BF16)

    n_rows = nh * N_CHUNKS
    gates = (gt_ref[...] + gb_ref[...][:, None]).reshape(n_rows * GATE_ROWS, L)
    gate_parts = _split3(gates)
    sel_r = lax.broadcasted_iota(jnp.int32, (n_rows, n_rows * GATE_ROWS), 0)
    sel_c = lax.broadcasted_iota(jnp.int32, (n_rows, n_rows * GATE_ROWS), 1)
    lane2 = lax.broadcasted_iota(jnp.int32, (n_rows, L), 1)
    chunk2 = lax.broadcasted_iota(jnp.int32, (n_rows, L), 0) % N_CHUNKS

    def gate_rows(kind):
        pick = jnp.where(sel_c == sel_r * GATE_ROWS + kind, 1.0, 0.0).astype(BF16)
        return sum(_dot(pick, p) for p in gate_parts)

    dirs = (0, 1)
    cum_ones = (ones_le, ones_ge)
    li = [gate_rows(2 * dr) for dr in dirs]
    lf = [_log_sigmoid(gate_rows(2 * dr + 1)) for dr in dirs]
    b = [sum(_dot(p, cum_ones[dr]) for p in _split3(lf[dr])) for dr in dirs]
    b_last = [jnp.broadcast_to(b[0][:, L - 1:L], b[0].shape), jnp.broadcast_to(b[1][:, 0:1], b[1].shape)]
    a = [b_last[dr] - b[dr] + li[dr] for dr in dirs]
    a_max = [jnp.broadcast_to(jnp.max(a[dr], axis=-1, keepdims=True), a[dr].shape) for dr in dirs]
    r = [li[dr] - b[dr] for dr in dirs]
    cm = list(r)
    for sh in (1, 2, 4, 8, 16, 32, 64):
        cm = [jnp.maximum(cm[0], jnp.where(lane2 >= sh, pltpu.roll(cm[0], sh, 1), -jnp.inf)),
              jnp.maximum(cm[1], jnp.where(lane2 < L - sh, pltpu.roll(cm[1], L - sh, 1), -jnp.inf))]
    sa, sb = list(b_last), list(a_max)
    for sh in (1, 2, 4, 8):
        ok = (chunk2 >= sh, chunk2 < N_CHUNKS - sh)
        shift = (sh, n_rows - sh)
        pa = [pltpu.roll(sa[dr], shift[dr], 0) for dr in dirs]
        pb = [pltpu.roll(sb[dr], shift[dr], 0) for dr in dirs]
        sb = [jnp.where(ok[dr], jnp.maximum(pb[dr] + sa[dr], sb[dr]), sb[dr]) for dr in dirs]
        sa = [jnp.where(ok[dr], pa[dr] + sa[dr], sa[dr]) for dr in dirs]
    m_new = [jnp.maximum(sa[dr], sb[dr]) for dr in dirs]
    m_prev = [jnp.where(chunk2 >= 1, pltpu.roll(m_new[0], 1, 0), 0.0),
              jnp.where(chunk2 < N_CHUNKS - 1, pltpu.roll(m_new[1], n_rows - 1, 0), 0.0)]
    m_t = [b[dr] + jnp.maximum(m_prev[dr], cm[dr]) for dr in dirs]
    for dr in dirs:
        for j, val in enumerate((b[dr] - m_t[dr], jnp.exp(b[dr] + m_prev[dr] - m_t[dr]), jnp.exp(-m_t[dr]),
                                 jnp.exp(a[dr] - m_new[dr]), r[dr],
                                 jnp.exp(b_last[dr] + m_prev[dr] - m_new[dr]))):
            for hh in range(nh):
                rows_ref[:, (2 * hh + dr) * ROWS_PER_CHAIN + j, :] = val[hh * N_CHUNKS:(hh + 1) * N_CHUNKS]

    cf_ref[...] = jnp.zeros(cf_ref.shape, F32)
    cb_ref[...] = jnp.zeros(cb_ref.shape, F32)

    chains = [(hh, direction) for hh in range(nh) for direction in range(2)]
    c_refs = (cf_ref, cb_ref)

    def chain_rows(i):
        chunks = (i, N_CHUNKS - 1 - i)
        tiles = (rows_ref[chunks[0]], rows_ref[chunks[1]])

        def row(n, j):
            hh, direction = chains[n]
            base = (2 * hh + direction) * ROWS_PER_CHAIN + j
            return tiles[direction][base:base + 1, :]

        return chunks, row

    def k_chunk(hh, c):
        return ks_ref[pl.ds(pl.multiple_of(c * L, L), L), hh * d:(hh + 1) * d]

    def intra_matmuls(i):
        chunks, row = chain_rows(i)
        r_rows = jnp.concatenate([row(n, 4) for n in range(len(chains))], axis=0)
        r_cols = sum(_dot_nt(eye, p) for p in _split3(r_rows))
        qtcs = [qt_ref[hh, chunks[direction]] for hh, direction in chains]
        kqs = [_dot(k_chunk(hh, chunks[direction]), qtcs[n]) for n, (hh, direction) in enumerate(chains)]
        return row, r_cols, qtcs, kqs

    def intra_store(ctx):
        row, r_cols, qtcs, kqs = ctx
        for n, (hh, direction) in enumerate(chains):
            st = kqs[n] * jnp.exp(jnp.where(masks[direction], r_cols[:, n:n + 1] + row(n, 0), -jnp.inf))
            wq = qtcs[n].astype(F32) * row(n, 1)
            rhs_ref[n] = jnp.concatenate([st.astype(BF16), wq.astype(BF16)], axis=0)

    def state_matmuls(i):
        chunks, row = chain_rows(i)
        vtas = [vt_ref[hh, chunks[direction]] for hh, direction in chains]
        states = [c_refs[direction][hh] for hh, direction in chains]
        nds = [_dot(jnp.concatenate([vtas[n], states[n].astype(BF16)], axis=1), rhs_ref[n])
               for n in range(len(chains))]
        return chunks, row, vtas, states, nds

    def state_finish(ctx, accumulate):
        chunks, row, vtas, states, nds = ctx
        for n, (hh, direction) in enumerate(chains):
            nd = nds[n]
            h = nd[:d, :] * (1.0 / jnp.maximum(jnp.abs(nd[d:d + 1, :]), row(n, 2)))
            if accumulate:
                out_ref[hh, chunks[direction]] += h
            else:
                out_ref[hh, chunks[direction]] = h
        for n, (hh, direction) in enumerate(chains):
            vw = (vtas[n].astype(F32) * row(n, 3)).astype(BF16)
            c_refs[direction][hh] = row(n, 5) * states[n] + _dot(vw, k_chunk(hh, chunks[direction]))

    def body(i, carry, accumulate):
        ahead = intra_matmuls(i + 1)
        cur = state_matmuls(i)
        intra_store(ahead)
        state_finish(cur, accumulate)
        return carry

    intra_store(intra_matmuls(0))
    lax.fori_loop(0, N_CHUNKS // 2, functools.partial(body, accumulate=False), 0)
    lax.fori_loop(N_CHUNKS // 2, N_CHUNKS - 1, functools.partial(body, accumulate=True), 0)
    state_finish(state_matmuls(N_CHUNKS - 1), True)


def _mlstm(qmt, proj, vta, gt, gbias, batch):
    t = proj.shape[0]
    per_chunk = lambda rows: pl.BlockSpec((ML_HEADS, N_CHUNKS, rows, ML_CHUNK), lambda b: (0, b, 0, 0))
    return pl.pallas_call(
        _mlstm_kernel,
        grid=(batch,),
        in_specs=[
            per_chunk(ML_HEAD_DIM),
            pl.BlockSpec((SEQ, ML_WIDTH), lambda b: (b, 0)),
            per_chunk(AUG_ROWS),
            pl.BlockSpec((ML_HEADS, N_CHUNKS, GATE_ROWS, LANES), lambda b: (0, b, 0, 0)),
            pl.BlockSpec((ML_HEADS, GATE_ROWS, 1), lambda b: (0, 0, 0)),
        ],
        out_specs=per_chunk(ML_HEAD_DIM),
        out_shape=jax.ShapeDtypeStruct((ML_HEADS, t // ML_CHUNK, ML_HEAD_DIM, ML_CHUNK), F32),
        scratch_shapes=[
            pltpu.VMEM((ML_HEADS, AUG_ROWS, ML_HEAD_DIM), F32),
            pltpu.VMEM((ML_HEADS, AUG_ROWS, ML_HEAD_DIM), F32),
            pltpu.VMEM((N_CHUNKS, N_CHAIN_ROWS, LANES), F32),
            pltpu.VMEM((2 * ML_HEADS, 2 * ML_CHUNK, ML_CHUNK), BF16),
        ],
        compiler_params=pltpu.CompilerParams(
            dimension_semantics=("arbitrary",), vmem_limit_bytes=VMEM_LIMIT),
        name="mlstm",
    )(qmt, proj, vta, gt, gbias)


def _ffn_kernel(att_ref, ht_ref, o_ref, ng_ref, x_ref, wo_ref, g2_ref, wup_ref, wdn_ref, gf_ref, out_ref,
                ml_ref):
    for hh in range(ML_HEADS):
        cols = slice(hh * ML_HEAD_DIM, (hh + 1) * ML_HEAD_DIM)
        for cc in range(ROW_TILE // ML_CHUNK):
            rows = slice(cc * ML_CHUNK, (cc + 1) * ML_CHUNK)
            h = _rms(ht_ref[hh, cc].T, ng_ref[:, cols])
            gate = o_ref[rows, cols].astype(F32)
            ml_ref[rows, cols] = (h * (1.0 / (1.0 + jnp.exp(-gate)))).astype(BF16)
    x1 = (x_ref[...] + _dot(att_ref[...], wo_ref[:ATT_WIDTH, :])
          + _dot(ml_ref[...], wo_ref[ATT_WIDTH:, :]))
    hn = _rms(x1, g2_ref[...]).astype(BF16)
    out_ref[...] = x1
    for c0 in range(0, D_FF, FF_CHUNK):
        hid = _dot(hn, wup_ref[:, c0:c0 + FF_CHUNK])
        act = jnp.square(jnp.maximum(hid, 0.0)).astype(BF16)
        out_ref[...] += _dot(act, wdn_ref[c0:c0 + FF_CHUNK, :])
    out_ref[...] = _rms(out_ref[...], gf_ref[...])


def _ffn(att, ht, proj, ng, x2, w_out, g2, w_up, w_down, gf):
    t = x2.shape[0]
    row = lambda w: pl.BlockSpec((ROW_TILE, w), lambda i: (i, 0))
    whole = lambda a: pl.BlockSpec(a.shape, lambda i: (0, 0), pipeline_mode=pl.Buffered(1))
    chunks = ROW_TILE // ML_CHUNK
    return pl.pallas_call(
        _ffn_kernel,
        grid=(t // ROW_TILE,),
        in_specs=[row(ATT_WIDTH),
                  pl.BlockSpec((ML_HEADS, chunks, ML_HEAD_DIM, ML_CHUNK), lambda i: (0, i, 0, 0)),
                  pl.BlockSpec((ROW_TILE, ML_WIDTH), lambda i: (i, OFF_OM // ML_WIDTH)),
                  whole(ng), row(D_MODEL), whole(w_out), whole(g2), whole(w_up), whole(w_down), whole(gf)],
        out_specs=row(D_MODEL),
        out_shape=jax.ShapeDtypeStruct((t, D_MODEL), F32),
        scratch_shapes=[pltpu.VMEM((ROW_TILE, ML_WIDTH), BF16)],
        compiler_params=pltpu.CompilerParams(
            dimension_semantics=("arbitrary",), vmem_limit_bytes=VMEM_LIMIT),
        name="outproj_ffn",
    )(att, ht, proj, ng, x2, w_out, g2, w_up, w_down, gf)


def _t5_bucket(rel):
    nb = REL_BUCKETS // 2
    max_exact = nb // 2
    ret = jnp.where(rel > 0, nb, 0)
    n = jnp.abs(rel)
    nf = jnp.maximum(n, 1).astype(jnp.float32)
    large = max_exact + (jnp.log(nf / max_exact) / math.log(REL_MAX_DIST / max_exact)
                         * (nb - max_exact)).astype(jnp.int32)
    large = jnp.minimum(large, nb - 1)
    return ret + jnp.where(n < max_exact, n, large)


def kernel(x, norm1_g, w_in, b_gates, conv_w, ml_norm_g, sink_logits, w_out, norm2_g, w_up, w_down,
           rel_bias, final_g):
    batch, seq, d_model = x.shape
    assert (seq, d_model) == (SEQ, D_MODEL) and w_in.shape[0] == 1
    x2 = x.reshape(batch * seq, d_model)

    w = w_in[0]
    att_cols = ATT_WIDTH + 2 * ATT_KV_WIDTH
    gate_cols = w[:, att_cols + 4 * ML_WIDTH:].reshape(d_model, 4, ML_HEADS).transpose(0, 2, 1)
    gate_cols = jnp.pad(gate_cols, ((0, 0), (0, 0), (0, GATE_ROWS - 4))).reshape(d_model, -1)
    gate_cols = jnp.pad(gate_cols, ((0, 0), (0, LANES - gate_cols.shape[1])))
    w_pad = jnp.concatenate([w[:, att_cols:att_cols + 4 * ML_WIDTH], w[:, ATT_WIDTH:att_cols], gate_cols],
                            axis=1).astype(BF16)
    slots = jnp.asarray(ATT_SLOT_HEADS)
    wq_t = w[:, :ATT_WIDTH].reshape(d_model, ATT_HEADS, ATT_HEAD_DIM)[:, slots, :]
    wq_t = wq_t.reshape(d_model, ATT_WIDTH).T.astype(BF16)
    gbias = jnp.pad(b_gates[0].reshape(4, ML_HEADS).T, ((0, 0), (0, GATE_ROWS - 4)))[..., None]

    proj, qt, gt, qmt, vta = _inproj(x2, norm1_g, w_pad, wq_t, conv_w[0])

    c = jnp.arange(3 * BLOCK)[:, None]
    r = jnp.arange(BLOCK)[None, :]
    bucket_t = _t5_bucket(c - BLOCK - r).astype(jnp.int32)
    bias = _bias_table(rel_bias.astype(F32), bucket_t)
    att = _attention(qt, proj, bias, sink_logits[0].astype(F32), batch)

    ht = _mlstm(qmt, proj, vta, gt, gbias.astype(F32), batch)

    wo = w_out[0]
    wo_att = wo[:ATT_WIDTH].reshape(ATT_HEADS, ATT_HEAD_DIM, d_model)[slots].reshape(ATT_WIDTH, d_model)
    wo = jnp.concatenate([wo_att, wo[ATT_WIDTH:]], axis=0)
    out = _ffn(att, ht, proj, ml_norm_g, x2, wo.astype(BF16), norm2_g, w_up[0].astype(BF16),
               w_down[0].astype(BF16), final_g.reshape(1, d_model))
    return out.reshape(batch, seq, d_model)
```

```python
import functools
import math

import jax
import jax.numpy as jnp
from jax import lax
from jax.experimental import pallas as pl
from jax.experimental.pallas import tpu as pltpu

D_MODEL = 1024
SEQ = 2048
ATT_HEADS = 8
ATT_KV_HEADS = 2
ATT_HEAD_DIM = 64
ATT_WIDTH = ATT_HEADS * ATT_HEAD_DIM
ATT_KV_WIDTH = ATT_KV_HEADS * ATT_HEAD_DIM
WINDOW = 128
BLOCK = 128
REL_BUCKETS = 32
REL_MAX_DIST = 128
ML_HEADS = 4
ML_HEAD_DIM = 128
ML_WIDTH = ML_HEADS * ML_HEAD_DIM
ML_CHUNK = 128
N_GATE_COLS = 4 * ML_HEADS
D_FF = 4 * D_MODEL
EPS = 1e-6

LANES = 128
N_CHUNKS = SEQ // ML_CHUNK
GATE_ROWS = 8
MAIN_WIDTH = 4 * ML_WIDTH + 2 * ATT_KV_WIDTH
PROJ_PAD = MAIN_WIDTH + LANES
OFF_OM = ML_WIDTH
OFF_KA = 2 * ML_WIDTH
OFF_VA = OFF_KA + ATT_KV_WIDTH
PROJ_OUT = OFF_VA + ATT_KV_WIDTH
HALO = 8
AUG_ROWS = ML_HEAD_DIM + 16

ROW_TILE = 1024
FF_CHUNK = 1024
VMEM_LIMIT = 56 * 1024 * 1024

F32 = jnp.float32
BF16 = jnp.bfloat16
NT_DIMS = (((1,), (1,)), ((), ()))


def _dot(a, b):
    return jnp.dot(a, b, preferred_element_type=F32)


def _dot_nt(a, b):
    return lax.dot_general(a, b, NT_DIMS, preferred_element_type=F32)


def _rms(x, g):
    ms = jnp.mean(x * x, axis=-1, keepdims=True)
    return x * lax.rsqrt(ms + EPS) * g


def _inproj_kernel(x_ref, xp_ref, xn_ref, g_ref, w_ref, wqt_ref, cw_ref,
                   proj_ref, qt_ref, gt_ref, qmt_ref, vta_ref, aq_ref, ak_ref, av_ref):
    i = pl.program_id(0)
    tiles_per_seq = SEQ // ROW_TILE
    has_prev = (i % tiles_per_seq) != 0
    has_next = (i % tiles_per_seq) != tiles_per_seq - 1
    g = g_ref[...]
    u = _rms(x_ref[...], g).astype(BF16)
    u_halo = _rms(jnp.concatenate([xp_ref[...], xn_ref[...]], axis=0), g).astype(BF16)
    u_ext = jnp.concatenate([u, u_halo], axis=0)
    row = lax.broadcasted_iota(jnp.int32, (ROW_TILE, ML_WIDTH), 0)

    def conv_silu(acc, w):
        cur = acc[0:ROW_TILE, :]
        halo_p = jnp.where(has_prev, acc[ROW_TILE + HALO - 1:ROW_TILE + HALO, :], 0.0)
        halo_n = jnp.where(has_next, acc[ROW_TILE + HALO:ROW_TILE + HALO + 1, :], 0.0)
        prev = jnp.where(row == 0, halo_p, pltpu.roll(cur, 1, 0))
        nxt = jnp.where(row == ROW_TILE - 1, halo_n, pltpu.roll(cur, ROW_TILE - 1, 0))
        y = w[0:1, :] * prev + w[1:2, :] * cur + w[2:3, :] * nxt
        return y * (1.0 / (1.0 + jnp.exp(-y)))

    aug_tail = jnp.where(lax.broadcasted_iota(jnp.int32, (AUG_ROWS - ML_HEAD_DIM, ML_CHUNK), 0) == 0,
                         1.0, 0.0).astype(BF16)
    tiles = [(hh, cc) for hh in range(ML_HEADS) for cc in range(ROW_TILE // ML_CHUNK)]

    def tile_t(a, hh, cc):
        return a[cc * ML_CHUNK:(cc + 1) * ML_CHUNK, hh * ML_HEAD_DIM:(hh + 1) * ML_HEAD_DIM].T.astype(BF16)

    aq_ref[...] = _dot(u_ext, w_ref[:, 0:ML_WIDTH])
    ak_ref[...] = _dot(u_ext, w_ref[:, ML_WIDTH:2 * ML_WIDTH])
    q = conv_silu(aq_ref, cw_ref[:, 0:ML_WIDTH])
    for hh, cc in tiles:
        qmt_ref[hh, cc] = tile_t(q, hh, cc)
    av_ref[...] = _dot(u, w_ref[:, 2 * ML_WIDTH:3 * ML_WIDTH])
    k = conv_silu(ak_ref, cw_ref[:, ML_WIDTH:2 * ML_WIDTH])
    proj_ref[:, 0:ML_WIDTH] = (k * (ML_HEAD_DIM ** -0.5)).astype(BF16)
    proj_ref[:, ML_WIDTH:2 * ML_WIDTH] = _dot(u, w_ref[:, 3 * ML_WIDTH:4 * ML_WIDTH]).astype(BF16)
    for hh, cc in tiles:
        vta_ref[hh, cc] = jnp.concatenate([tile_t(av_ref, hh, cc), aug_tail], axis=0)
    proj_ref[:, 2 * ML_WIDTH:PROJ_OUT] = _dot(u, w_ref[:, 4 * ML_WIDTH:MAIN_WIDTH]).astype(BF16)
    qt = (_dot_nt(wqt_ref[...], u) * (ATT_HEAD_DIM ** -0.5 * LOG2_E)).astype(BF16)
    for cc in range(ROW_TILE // BLOCK):
        qt_ref[cc] = qt[:, cc * BLOCK:(cc + 1) * BLOCK]
    gates = _dot(u, w_ref[:, MAIN_WIDTH:PROJ_PAD])
    for cc in range(ROW_TILE // ML_CHUNK):
        gt = gates[cc * ML_CHUNK:(cc + 1) * ML_CHUNK, :].T
        for h in range(ML_HEADS):
            gt_ref[h, cc] = gt[h * GATE_ROWS:(h + 1) * GATE_ROWS, :]


def _inproj(x2, g1, w_pad, wq_t, conv_w):
    t = x2.shape[0]
    halo_blocks = ROW_TILE // HALO
    chunks = ROW_TILE // ML_CHUNK
    return pl.pallas_call(
        _inproj_kernel,
        grid=(t // ROW_TILE,),
        in_specs=[
            pl.BlockSpec((ROW_TILE, D_MODEL), lambda i: (i, 0)),
            pl.BlockSpec((HALO, D_MODEL), lambda i: (jnp.maximum(i * halo_blocks - 1, 0), 0)),
            pl.BlockSpec((HALO, D_MODEL), lambda i: (jnp.minimum((i + 1) * halo_blocks, t // HALO - 1), 0)),
            pl.BlockSpec((1, D_MODEL), lambda i: (0, 0)),
            pl.BlockSpec((D_MODEL, PROJ_PAD), lambda i: (0, 0), pipeline_mode=pl.Buffered(1)),
            pl.BlockSpec((ATT_WIDTH, D_MODEL), lambda i: (0, 0), pipeline_mode=pl.Buffered(1)),
            pl.BlockSpec((3, 2 * ML_WIDTH), lambda i: (0, 0)),
        ],
        out_specs=[
            pl.BlockSpec((ROW_TILE, PROJ_OUT), lambda i: (i, 0)),
            pl.BlockSpec((ROW_TILE // BLOCK, ATT_WIDTH, BLOCK), lambda i: (i, 0, 0)),
            pl.BlockSpec((ML_HEADS, chunks, GATE_ROWS, LANES), lambda i: (0, i, 0, 0)),
            pl.BlockSpec((ML_HEADS, chunks, ML_HEAD_DIM, ML_CHUNK), lambda i: (0, i, 0, 0)),
            pl.BlockSpec((ML_HEADS, chunks, AUG_ROWS, ML_CHUNK), lambda i: (0, i, 0, 0)),
        ],
        out_shape=[
            jax.ShapeDtypeStruct((t, PROJ_OUT), BF16),
            jax.ShapeDtypeStruct((t // BLOCK, ATT_WIDTH, BLOCK), BF16),
            jax.ShapeDtypeStruct((ML_HEADS, t // ML_CHUNK, GATE_ROWS, LANES), F32),
            jax.ShapeDtypeStruct((ML_HEADS, t // ML_CHUNK, ML_HEAD_DIM, ML_CHUNK), BF16),
            jax.ShapeDtypeStruct((ML_HEADS, t // ML_CHUNK, AUG_ROWS, ML_CHUNK), BF16),
        ],
        scratch_shapes=[
            pltpu.VMEM((ROW_TILE + 2 * HALO, ML_WIDTH), F32),
            pltpu.VMEM((ROW_TILE + 2 * HALO, ML_WIDTH), F32),
            pltpu.VMEM((ROW_TILE, ML_WIDTH), F32),
        ],
        compiler_params=pltpu.CompilerParams(
            dimension_semantics=("arbitrary",), vmem_limit_bytes=VMEM_LIMIT),
        name="inproj",
    )(x2, x2, x2, g1, w_pad, wq_t, conv_w)


def _bias_kernel(rb_ref, bucket_ref, out_ref):
    bucket = bucket_ref[...]
    c = lax.broadcasted_iota(jnp.int32, bucket.shape, 0)
    r = lax.broadcasted_iota(jnp.int32, bucket.shape, 1)
    valid = jnp.abs(c - BLOCK - r) <= WINDOW
    for h in range(ATT_HEADS):
        acc = jnp.zeros(bucket.shape, F32)
        for b in range(REL_BUCKETS):
            acc = jnp.where(bucket == b, rb_ref[b, h], acc)
        out_ref[h] = jnp.where(valid, acc * LOG2_E, -jnp.inf)


def _bias_table(rel_bias, bucket_t):
    return pl.pallas_call(
        _bias_kernel,
        in_specs=[
            pl.BlockSpec(memory_space=pltpu.SMEM),
            pl.BlockSpec(memory_space=pltpu.VMEM),
        ],
        out_specs=pl.BlockSpec(memory_space=pltpu.VMEM),
        out_shape=jax.ShapeDtypeStruct((ATT_HEADS, 3 * BLOCK, BLOCK), F32),
        name="bias_table",
    )(rel_bias, bucket_t)


N_QBLOCKS = SEQ // BLOCK
ATT_SLOT_HEADS = tuple(p // 2 + 4 * (p % 2) for p in range(ATT_HEADS))


BAND = 3 * BLOCK
ATT_AUG_ROWS = LANES + 16
LOG2_E = math.log2(math.e)


def _attn_kernel(sink_ref, qt_ref, k_ref, v_ref, bias_ref, out_ref,
                 kz_ref, vtz_ref, sa_ref, sb_ref, pa_ref, pb_ref, la_ref, lb_ref):
    half = ATT_HEAD_DIM
    kf = k_ref[...].astype(F32)
    lane_s = lax.broadcasted_iota(jnp.int32, kf.shape, 1)
    zero_blk = jnp.zeros((BLOCK, LANES), BF16)
    row_t = lax.broadcasted_iota(jnp.int32, (ATT_AUG_ROWS - LANES, BLOCK), 0)
    ones_rows = [jnp.where(row_t == kv, 1.0, 0.0).astype(BF16) for kv in range(2)]
    for kv in range(2):
        kz_ref[kv, 0:BLOCK, :] = zero_blk
        kz_ref[kv, SEQ + BLOCK:SEQ + 2 * BLOCK, :] = zero_blk
        vtz_ref[kv, 0] = jnp.concatenate([zero_blk, ones_rows[kv]], axis=0)
        vtz_ref[kv, N_QBLOCKS + 1] = jnp.concatenate([zero_blk, ones_rows[kv]], axis=0)
    kz_ref[0, BLOCK:SEQ + BLOCK, :] = jnp.where(lane_s < half, kf, 0.0).astype(BF16)
    kz_ref[1, BLOCK:SEQ + BLOCK, :] = jnp.where(lane_s < half, 0.0, kf).astype(BF16)
    row_b = lax.broadcasted_iota(jnp.int32, (LANES, BLOCK), 0)
    for jb in range(N_QBLOCKS):
        vt = v_ref[jb * BLOCK:(jb + 1) * BLOCK, :].astype(F32).T
        vtz_ref[0, jb + 1] = jnp.concatenate([jnp.where(row_b < half, vt, 0.0).astype(BF16), ones_rows[0]], axis=0)
        vtz_ref[1, jb + 1] = jnp.concatenate([jnp.where(row_b < half, 0.0, vt).astype(BF16), ones_rows[1]], axis=0)
    row_o = lax.broadcasted_iota(jnp.int32, (LANES, 2 * BLOCK), 0)
    neg_blk = jnp.full((BLOCK, BLOCK), -jnp.inf, F32)

    def start(j):
        return j * BLOCK if isinstance(j, int) else pl.multiple_of(j * BLOCK, BLOCK)

    def stage_scores(j, s_ref):
        qt = qt_ref[j]
        k0 = start(j)
        kzb = jnp.concatenate([kz_ref[0, pl.ds(k0, BAND), :], kz_ref[1, pl.ds(k0, BAND), :]], axis=0)
        for pair in range(2):
            rows = 2 * LANES * pair
            rhs = jnp.concatenate([qt[rows:rows + LANES, :], qt[rows + LANES:rows + 2 * LANES, :]], axis=1)
            s_ref[pair] = _dot(kzb, rhs)

    def stage_softmax(s_ref, p_ref, l_ref, edge):
        for pair in range(2):
            for kv in range(2):
                for t in range(2):
                    h = 2 * pair + t + 4 * kv
                    rs = slice(kv * BAND, (kv + 1) * BAND)
                    cs = slice(t * BLOCK, (t + 1) * BLOCK)
                    if edge < 0:
                        bias = jnp.concatenate([neg_blk, bias_ref[h, BLOCK:BAND, :]], axis=0)
                    elif edge > 0:
                        bias = jnp.concatenate([bias_ref[h, 0:2 * BLOCK, :], neg_blk], axis=0)
                    else:
                        bias = bias_ref[h]
                    s = s_ref[pair, rs, cs] + bias
                    sink = sink_ref[h] * LOG2_E
                    m = jnp.maximum(jnp.max(s, axis=0, keepdims=True), sink)
                    l_ref[pair, kv:kv + 1, cs] = jnp.exp2(sink - m)
                    p_ref[pair, rs, cs] = jnp.exp2(s - m).astype(BF16)

    def stage_pv(j, p_ref, l_ref):
        q0 = start(j)
        vtb = jnp.concatenate([vtz_ref[kv, j + i] for kv in range(2) for i in range(3)], axis=1)
        for pair in range(2):
            ot = _dot(vtb, p_ref[pair])
            inv = 1.0 / (ot[LANES:LANES + 2, :] + l_ref[pair, 0:2, :])
            ot = ot[:LANES, :] * jnp.where(row_o < half, inv[0:1, :], inv[1:2, :])
            for t in range(2):
                c0 = (2 * pair + t) * LANES
                out_ref[pl.ds(q0, BLOCK), c0:c0 + LANES] = ot[:, t * BLOCK:(t + 1) * BLOCK].T.astype(BF16)

    stage_scores(0, sa_ref)
    stage_scores(1, sb_ref)
    stage_softmax(sa_ref, pa_ref, la_ref, -1)

    def two_blocks(ii, carry):
        j = 2 * ii
        stage_pv(j - 2, pa_ref, la_ref)
        stage_scores(j, sa_ref)
        stage_softmax(sb_ref, pb_ref, lb_ref, 0)
        stage_pv(j - 1, pb_ref, lb_ref)
        stage_scores(j + 1, sb_ref)
        stage_softmax(sa_ref, pa_ref, la_ref, 0)
        return carry

    lax.fori_loop(1, N_QBLOCKS // 2, two_blocks, 0)
    stage_softmax(sb_ref, pb_ref, lb_ref, 1)
    stage_pv(N_QBLOCKS - 2, pa_ref, la_ref)
    stage_pv(N_QBLOCKS - 1, pb_ref, lb_ref)


def _attention(qt, proj, bias, sink, batch):
    t = proj.shape[0]
    kv_spec = lambda off: pl.BlockSpec((SEQ, ATT_KV_WIDTH), lambda b: (b, off // ATT_KV_WIDTH))
    return pl.pallas_call(
        _attn_kernel,
        grid=(batch,),
        in_specs=[
            pl.BlockSpec(memory_space=pltpu.SMEM),
            pl.BlockSpec((N_QBLOCKS, ATT_WIDTH, BLOCK), lambda b: (b, 0, 0)),
            kv_spec(OFF_KA),
            kv_spec(OFF_VA),
            pl.BlockSpec((ATT_HEADS, 3 * BLOCK, BLOCK), lambda b: (0, 0, 0)),
        ],
        out_specs=pl.BlockSpec((SEQ, ATT_WIDTH), lambda b: (b, 0)),
        out_shape=jax.ShapeDtypeStruct((t, ATT_WIDTH), BF16),
        scratch_shapes=[
            pltpu.VMEM((2, SEQ + 2 * BLOCK, ATT_KV_WIDTH), BF16),
            pltpu.VMEM((2, N_QBLOCKS + 2, ATT_AUG_ROWS, BLOCK), BF16),
            pltpu.VMEM((2, 2 * BAND, 2 * BLOCK), F32),
            pltpu.VMEM((2, 2 * BAND, 2 * BLOCK), F32),
            pltpu.VMEM((2, 2 * BAND, 2 * BLOCK), BF16),
            pltpu.VMEM((2, 2 * BAND, 2 * BLOCK), BF16),
            pltpu.VMEM((2, 8, 2 * BLOCK), F32),
            pltpu.VMEM((2, 8, 2 * BLOCK), F32),
        ],
        compiler_params=pltpu.CompilerParams(
            dimension_semantics=("arbitrary",), vmem_limit_bytes=VMEM_LIMIT),
        name="attention",
    )(sink, qt, proj, proj, bias)


def _log_sigmoid(x):
    return jnp.minimum(x, 0.0) - jnp.log1p(jnp.exp(-jnp.abs(x)))


def _split3(x):
    hi = x.astype(BF16)
    r1 = x - hi.astype(F32)
    mid = r1.astype(BF16)
    lo = (r1 - mid.astype(F32)).astype(BF16)
    return hi, mid, lo


ROWS_PER_CHAIN = 6
N_CHAIN_ROWS = 2 * ML_HEADS * ROWS_PER_CHAIN


def _mlstm_kernel(qt_ref, ks_ref, vt_ref, gt_ref, gb_ref, out_ref, cf_ref, cb_ref, rows_ref, rhs_ref):
    L = ML_CHUNK
    d = ML_HEAD_DIM
    nh = ML_HEADS
    si = lax.broadcasted_iota(jnp.int32, (L, L), 0)
    ti = lax.broadcasted_iota(jnp.int32, (L, L), 1)
    masks = (si <= ti, si >= ti)
    ones_le = jnp.where(si <= ti, 1.0, 0.0).astype(BF16)
    ones_ge = jnp.where(si >= ti, 1.0, 0.0).astype(BF16)
    eye = jnp.where(si == ti, 1.0, 0.0).astype(BF16)

    n_rows = nh * N_CHUNKS
    gates = (gt_ref[...] + gb_ref[...][:, None]).reshape(n_rows * GATE_ROWS, L)
    gate_parts = _split3(gates)
    sel_r = lax.broadcasted_iota(jnp.int32, (n_rows, n_rows * GATE_ROWS), 0)
    sel_c = lax.broadcasted_iota(jnp.int32, (n_rows, n_rows * GATE_ROWS), 1)
    lane2 = lax.broadcasted_iota(jnp.int32, (n_rows, L), 1)
    chunk2 = lax.broadcasted_iota(jnp.int32, (n_rows, L), 0) % N_CHUNKS

    def gate_rows(kind):
        pick = jnp.where(sel_c == sel_r * GATE_ROWS + kind, 1.0, 0.0).astype(BF16)
        return sum(_dot(pick, p) for p in gate_parts)

    dirs = (0, 1)
    cum_ones = (ones_le, ones_ge)
    li = [gate_rows(2 * dr) for dr in dirs]
    lf = [_log_sigmoid(gate_rows(2 * dr + 1)) for dr in dirs]
    b = [sum(_dot(p, cum_ones[dr]) for p in _split3(lf[dr])) for dr in dirs]
    b_last = [jnp.broadcast_to(b[0][:, L - 1:L], b[0].shape), jnp.broadcast_to(b[1][:, 0:1], b[1].shape)]
    a = [b_last[dr] - b[dr] + li[dr] for dr in dirs]
    a_max = [jnp.broadcast_to(jnp.max(a[dr], axis=-1, keepdims=True), a[dr].shape) for dr in dirs]
    r = [li[dr] - b[dr] for dr in dirs]
    cm = list(r)
    for sh in (1, 2, 4, 8, 16, 32, 64):
        cm = [jnp.maximum(cm[0], jnp.where(lane2 >= sh, pltpu.roll(cm[0], sh, 1), -jnp.inf)),
              jnp.maximum(cm[1], jnp.where(lane2 < L - sh, pltpu.roll(cm[1], L - sh, 1), -jnp.inf))]
    sa, sb = list(b_last), list(a_max)
    for sh in (1, 2, 4, 8):
        ok = (chunk2 >= sh, chunk2 < N_CHUNKS - sh)
        shift = (sh, n_rows - sh)
        pa = [pltpu.roll(sa[dr], shift[dr], 0) for dr in dirs]
        pb = [pltpu.roll(sb[dr], shift[dr], 0) for dr in dirs]
        sb = [jnp.where(ok[dr], jnp.maximum(pb[dr] + sa[dr], sb[dr]), sb[dr]) for dr in dirs]
        sa = [jnp.where(ok[dr], pa[dr] + sa[dr], sa[dr]) for dr in dirs]
    m_new = [jnp.maximum(sa[dr], sb[dr]) for dr in dirs]
    m_prev = [jnp.where(chunk2 >= 1, pltpu.roll(m_new[0], 1, 0), 0.0),
              jnp.where(chunk2 < N_CHUNKS - 1, pltpu.roll(m_new[1], n_rows - 1, 0), 0.0)]
    m_t = [b[dr] + jnp.maximum(m_prev[dr], cm[dr]) for dr in dirs]
    for dr in dirs:
        for j, val in enumerate((b[dr] - m_t[dr], jnp.exp(b[dr] + m_prev[dr] - m_t[dr]), jnp.exp(-m_t[dr]),
                                 jnp.exp(a[dr] - m_new[dr]), r[dr],
                                 jnp.exp(b_last[dr] + m_prev[dr] - m_new[dr]))):
            for hh in range(nh):
                rows_ref[:, (2 * hh + dr) * ROWS_PER_CHAIN + j, :] = val[hh * N_CHUNKS:(hh + 1) * N_CHUNKS]

    cf_ref[...] = jnp.zeros(cf_ref.shape, F32)
    cb_ref[...] = jnp.zeros(cb_ref.shape, F32)

    chains = [(hh, direction) for hh in range(nh) for direction in range(2)]
    c_refs = (cf_ref, cb_ref)

    def chain_rows(i):
        chunks = (i, N_CHUNKS - 1 - i)
        tiles = (rows_ref[chunks[0]], rows_ref[chunks[1]])

        def row(n, j):
            hh, direction = chains[n]
            base = (2 * hh + direction) * ROWS_PER_CHAIN + j
            return tiles[direction][base:base + 1, :]

        return chunks, row

    def k_chunk(hh, c):
        return ks_ref[pl.ds(pl.multiple_of(c * L, L), L), hh * d:(hh + 1) * d]

    def intra_matmuls(i):
        chunks, row = chain_rows(i)
        r_rows = jnp.concatenate([row(n, 4) for n in range(len(chains))], axis=0)
        r_cols = sum(_dot_nt(eye, p) for p in _split3(r_rows))
        qtcs = [qt_ref[hh, chunks[direction]] for hh, direction in chains]
        kqs = [_dot(k_chunk(hh, chunks[direction]), qtcs[n]) for n, (hh, direction) in enumerate(chains)]
        return row, r_cols, qtcs, kqs

    def intra_store(ctx):
        row, r_cols, qtcs, kqs = ctx
        for n, (hh, direction) in enumerate(chains):
            st = kqs[n] * jnp.exp(jnp.where(masks[direction], r_cols[:, n:n + 1] + row(n, 0), -jnp.inf))
            wq = qtcs[n].astype(F32) * row(n, 1)
            rhs_ref[n] = jnp.concatenate([st.astype(BF16), wq.astype(BF16)], axis=0)

    def state_matmuls(i):
        chunks, row = chain_rows(i)
        vtas = [vt_ref[hh, chunks[direction]] for hh, direction in chains]
        states = [c_refs[direction][hh] for hh, direction in chains]
        nds = [_dot(jnp.concatenate([vtas[n], states[n].astype(BF16)], axis=1), rhs_ref[n])
               for n in range(len(chains))]
        return chunks, row, vtas, states, nds

    def state_finish(ctx, accumulate):
        chunks, row, vtas, states, nds = ctx
        for n, (hh, direction) in enumerate(chains):
            nd = nds[n]
            h = nd[:d, :] * (1.0 / jnp.maximum(jnp.abs(nd[d:d + 1, :]), row(n, 2)))
            if accumulate:
                out_ref[hh, chunks[direction]] += h
            else:
                out_ref[hh, chunks[direction]] = h
        for n, (hh, direction) in enumerate(chains):
            vw = (vtas[n].astype(F32) * row(n, 3)).astype(BF16)
            c_refs[direction][hh] = row(n, 5) * states[n] + _dot(vw, k_chunk(hh, chunks[direction]))

    def body(i, carry, accumulate):
        ahead = intra_matmuls(i + 1)
        cur = state_matmuls(i)
        intra_store(ahead)
        state_finish(cur, accumulate)
        return carry

    intra_store(intra_matmuls(0))
    lax.fori_loop(0, N_CHUNKS // 2, functools.partial(body, accumulate=False), 0)
    lax.fori_loop(N_CHUNKS // 2, N_CHUNKS - 1, functools.partial(body, accumulate=True), 0)
    state_finish(state_matmuls(N_CHUNKS - 1), True)


def _mlstm(qmt, proj, vta, gt, gbias, batch):
    t = proj.shape[0]
    per_chunk = lambda rows: pl.BlockSpec((ML_HEADS, N_CHUNKS, rows, ML_CHUNK), lambda b: (0, b, 0, 0))
    return pl.pallas_call(
        _mlstm_kernel,
        grid=(batch,),
        in_specs=[
            per_chunk(ML_HEAD_DIM),
            pl.BlockSpec((SEQ, ML_WIDTH), lambda b: (b, 0)),
            per_chunk(AUG_ROWS),
            pl.BlockSpec((ML_HEADS, N_CHUNKS, GATE_ROWS, LANES), lambda b: (0, b, 0, 0)),
            pl.BlockSpec((ML_HEADS, GATE_ROWS, 1), lambda b: (0, 0, 0)),
        ],
        out_specs=per_chunk(ML_HEAD_DIM),
        out_shape=jax.ShapeDtypeStruct((ML_HEADS, t // ML_CHUNK, ML_HEAD_DIM, ML_CHUNK), F32),
        scratch_shapes=[
            pltpu.VMEM((ML_HEADS, AUG_ROWS, ML_HEAD_DIM), F32),
            pltpu.VMEM((ML_HEADS, AUG_ROWS, ML_HEAD_DIM), F32),
            pltpu.VMEM((N_CHUNKS, N_CHAIN_ROWS, LANES), F32),
            pltpu.VMEM((2 * ML_HEADS, 2 * ML_CHUNK, ML_CHUNK), BF16),
        ],
        compiler_params=pltpu.CompilerParams(
            dimension_semantics=("arbitrary",), vmem_limit_bytes=VMEM_LIMIT),
        name="mlstm",
    )(qmt, proj, vta, gt, gbias)


def _ffn_kernel(att_ref, ht_ref, o_ref, ng_ref, x_ref, wo_ref, g2_ref, wup_ref, wdn_ref, gf_ref, out_ref,
                ml_ref):
    for hh in range(ML_HEADS):
        cols = slice(hh * ML_HEAD_DIM, (hh + 1) * ML_HEAD_DIM)
        for cc in range(ROW_TILE // ML_CHUNK):
            rows = slice(cc * ML_CHUNK, (cc + 1) * ML_CHUNK)
            h = _rms(ht_ref[hh, cc].T, ng_ref[:, cols])
            gate = o_ref[rows, cols].astype(F32)
            ml_ref[rows, cols] = (h * (1.0 / (1.0 + jnp.exp(-gate)))).astype(BF16)
    x1 = (x_ref[...] + _dot(att_ref[...], wo_ref[:ATT_WIDTH, :])
          + _dot(ml_ref[...], wo_ref[ATT_WIDTH:, :]))
    hn = _rms(x1, g2_ref[...]).astype(BF16)
    out_ref[...] = x1
    for c0 in range(0, D_FF, FF_CHUNK):
        hid = _dot(hn, wup_ref[:, c0:c0 + FF_CHUNK])
        act = jnp.square(jnp.maximum(hid, 0.0)).astype(BF16)
        out_ref[...] += _dot(act, wdn_ref[c0:c0 + FF_CHUNK, :])
    out_ref[...] = _rms(out_ref[...], gf_ref[...])


def _ffn(att, ht, proj, ng, x2, w_out, g2, w_up, w_down, gf):
    t = x2.shape[0]
    row = lambda w: pl.BlockSpec((ROW_TILE, w), lambda i: (i, 0))
    whole = lambda a: pl.BlockSpec(a.shape, lambda i: (0, 0), pipeline_mode=pl.Buffered(1))
    chunks = ROW_TILE // ML_CHUNK
    return pl.pallas_call(
        _ffn_kernel,
        grid=(t // ROW_TILE,),
        in_specs=[row(ATT_WIDTH),
                  pl.BlockSpec((ML_HEADS, chunks, ML_HEAD_DIM, ML_CHUNK), lambda i: (0, i, 0, 0)),
                  pl.BlockSpec((ROW_TILE, ML_WIDTH), lambda i: (i, OFF_OM // ML_WIDTH)),
                  whole(ng), row(D_MODEL), whole(w_out), whole(g2), whole(w_up), whole(w_down), whole(gf)],
        out_specs=row(D_MODEL),
        out_shape=jax.ShapeDtypeStruct((t, D_MODEL), F32),
        scratch_shapes=[pltpu.VMEM((ROW_TILE, ML_WIDTH), BF16)],
        compiler_params=pltpu.CompilerParams(
            dimension_semantics=("arbitrary",), vmem_limit_bytes=VMEM_LIMIT),
        name="outproj_ffn",
    )(att, ht, proj, ng, x2, w_out, g2, w_up, w_down, gf)


def _t5_bucket(rel):
    nb = REL_BUCKETS // 2
    max_exact = nb // 2
    ret = jnp.where(rel > 0, nb, 0)
    n = jnp.abs(rel)
    nf = jnp.maximum(n, 1).astype(jnp.float32)
    large = max_exact + (jnp.log(nf / max_exact) / math.log(REL_MAX_DIST / max_exact)
                         * (nb - max_exact)).astype(jnp.int32)
    large = jnp.minimum(large, nb - 1)
    return ret + jnp.where(n < max_exact, n, large)


def kernel(x, norm1_g, w_in, b_gates, conv_w, ml_norm_g, sink_logits, w_out, norm2_g, w_up, w_down,
           rel_bias, final_g):
    batch, seq, d_model = x.shape
    assert (seq, d_model) == (SEQ, D_MODEL) and w_in.shape[0] == 1
    x2 = x.reshape(batch * seq, d_model)

    w = w_in[0]
    att_cols = ATT_WIDTH + 2 * ATT_KV_WIDTH
    gate_cols = w[:, att_cols + 4 * ML_WIDTH:].reshape(d_model, 4, ML_HEADS).transpose(0, 2, 1)
    gate_cols = jnp.pad(gate_cols, ((0, 0), (0, 0), (0, GATE_ROWS - 4))).reshape(d_model, -1)
    gate_cols = jnp.pad(gate_cols, ((0, 0), (0, LANES - gate_cols.shape[1])))
    w_pad = jnp.concatenate([w[:, att_cols:att_cols + 4 * ML_WIDTH], w[:, ATT_WIDTH:att_cols], gate_cols],
                            axis=1).astype(BF16)
    slots = jnp.asarray(ATT_SLOT_HEADS)
    wq_t = w[:, :ATT_WIDTH].reshape(d_model, ATT_HEADS, ATT_HEAD_DIM)[:, slots, :]
    wq_t = wq_t.reshape(d_model, ATT_WIDTH).T.astype(BF16)
    gbias = jnp.pad(b_gates[0].reshape(4, ML_HEADS).T, ((0, 0), (0, GATE_ROWS - 4)))[..., None]

    proj, qt, gt, qmt, vta = _inproj(x2, norm1_g, w_pad, wq_t, conv_w[0])

    c = jnp.arange(3 * BLOCK)[:, None]
    r = jnp.arange(BLOCK)[None, :]
    bucket_t = _t5_bucket(c - BLOCK - r).astype(jnp.int32)
    bias = _bias_table(rel_bias.astype(F32), bucket_t)
    att = _attention(qt, proj, bias, sink_logits[0].astype(F32), batch)

    ht = _mlstm(qmt, proj, vta, gt, gbias.astype(F32), batch)

    wo = w_out[0]
    wo_att = wo[:ATT_WIDTH].reshape(ATT_HEADS, ATT_HEAD_DIM, d_model)[slots].reshape(ATT_WIDTH, d_model)
    wo = jnp.concatenate([wo_att, wo[ATT_WIDTH:]], axis=0)
    out = _ffn(att, ht, proj, ml_norm_g, x2, wo.astype(BF16), norm2_g, w_up[0].astype(BF16),
               w_down[0].astype(BF16), final_g.reshape(1, d_model))
    return out.reshape(batch, seq, d_model)
```

```python
import functools
import math

import jax
import jax.numpy as jnp
from jax import lax
from jax.experimental import pallas as pl
from jax.experimental.pallas import tpu as pltpu

D_MODEL = 1024
SEQ = 2048
ATT_HEADS = 8
ATT_KV_HEADS = 2
ATT_HEAD_DIM = 64
ATT_WIDTH = ATT_HEADS * ATT_HEAD_DIM
ATT_KV_WIDTH = ATT_KV_HEADS * ATT_HEAD_DIM
WINDOW = 128
BLOCK = 128
REL_BUCKETS = 32
REL_MAX_DIST = 128
ML_HEADS = 4
ML_HEAD_DIM = 128
ML_WIDTH = ML_HEADS * ML_HEAD_DIM
ML_CHUNK = 128
N_GATE_COLS = 4 * ML_HEADS
D_FF = 4 * D_MODEL
EPS = 1e-6

LANES = 128
N_CHUNKS = SEQ // ML_CHUNK
GATE_ROWS = 8
MAIN_WIDTH = 4 * ML_WIDTH + 2 * ATT_KV_WIDTH
PROJ_PAD = MAIN_WIDTH + LANES
OFF_OM = ML_WIDTH
OFF_KA = 2 * ML_WIDTH
OFF_VA = OFF_KA + ATT_KV_WIDTH
PROJ_OUT = OFF_VA + ATT_KV_WIDTH
HALO = 8
AUG_ROWS = ML_HEAD_DIM + 16

ROW_TILE = 1024
FF_CHUNK = 1024
VMEM_LIMIT = 56 * 1024 * 1024

F32 = jnp.float32
BF16 = jnp.bfloat16
NT_DIMS = (((1,), (1,)), ((), ()))


def _dot(a, b):
    return jnp.dot(a, b, preferred_element_type=F32)


def _dot_nt(a, b):
    return lax.dot_general(a, b, NT_DIMS, preferred_element_type=F32)


def _rms(x, g):
    ms = jnp.mean(x * x, axis=-1, keepdims=True)
    return x * lax.rsqrt(ms + EPS) * g


def _inproj_kernel(x_ref, xp_ref, xn_ref, g_ref, w_ref, wqt_ref, cw_ref,
                   proj_ref, qt_ref, gt_ref, qmt_ref, vta_ref, aq_ref, ak_ref, av_ref):
    i = pl.program_id(0)
    tiles_per_seq = SEQ // ROW_TILE
    has_prev = (i % tiles_per_seq) != 0
    has_next = (i % tiles_per_seq) != tiles_per_seq - 1
    g = g_ref[...]
    u = _rms(x_ref[...], g).astype(BF16)
    u_halo = _rms(jnp.concatenate([xp_ref[...], xn_ref[...]], axis=0), g).astype(BF16)
    u_ext = jnp.concatenate([u, u_halo], axis=0)
    row = lax.broadcasted_iota(jnp.int32, (ROW_TILE, ML_WIDTH), 0)

    def conv_silu(acc, w):
        cur = acc[0:ROW_TILE, :]
        halo_p = jnp.where(has_prev, acc[ROW_TILE + HALO - 1:ROW_TILE + HALO, :], 0.0)
        halo_n = jnp.where(has_next, acc[ROW_TILE + HALO:ROW_TILE + HALO + 1, :], 0.0)
        prev = jnp.where(row == 0, halo_p, pltpu.roll(cur, 1, 0))
        nxt = jnp.where(row == ROW_TILE - 1, halo_n, pltpu.roll(cur, ROW_TILE - 1, 0))
        y = w[0:1, :] * prev + w[1:2, :] * cur + w[2:3, :] * nxt
        return y * (1.0 / (1.0 + jnp.exp(-y)))

    aug_tail = jnp.where(lax.broadcasted_iota(jnp.int32, (AUG_ROWS - ML_HEAD_DIM, ML_CHUNK), 0) == 0,
                         1.0, 0.0).astype(BF16)
    tiles = [(hh, cc) for hh in range(ML_HEADS) for cc in range(ROW_TILE // ML_CHUNK)]

    def tile_t(a, hh, cc):
        return a[cc * ML_CHUNK:(cc + 1) * ML_CHUNK, hh * ML_HEAD_DIM:(hh + 1) * ML_HEAD_DIM].T.astype(BF16)

    aq_ref[...] = _dot(u_ext, w_ref[:, 0:ML_WIDTH])
    ak_ref[...] = _dot(u_ext, w_ref[:, ML_WIDTH:2 * ML_WIDTH])
    q = conv_silu(aq_ref, cw_ref[:, 0:ML_WIDTH])
    for hh, cc in tiles:
        qmt_ref[hh, cc] = tile_t(q, hh, cc)
    av_ref[...] = _dot(u, w_ref[:, 2 * ML_WIDTH:3 * ML_WIDTH])
    k = conv_silu(ak_ref, cw_ref[:, ML_WIDTH:2 * ML_WIDTH])
    proj_ref[:, 0:ML_WIDTH] = k.astype(BF16)
    proj_ref[:, ML_WIDTH:2 * ML_WIDTH] = _dot(u, w_ref[:, 3 * ML_WIDTH:4 * ML_WIDTH]).astype(BF16)
    for hh, cc in tiles:
        vta_ref[hh, cc] = jnp.concatenate([tile_t(av_ref, hh, cc), aug_tail], axis=0)
    proj_ref[:, 2 * ML_WIDTH:PROJ_OUT] = _dot(u, w_ref[:, 4 * ML_WIDTH:MAIN_WIDTH]).astype(BF16)
    qt = (_dot_nt(wqt_ref[...], u) * (ATT_HEAD_DIM ** -0.5 * LOG2_E)).astype(BF16)
    for cc in range(ROW_TILE // BLOCK):
        qt_ref[cc] = qt[:, cc * BLOCK:(cc + 1) * BLOCK]
    gates = _dot(u, w_ref[:, MAIN_WIDTH:PROJ_PAD])
    for cc in range(ROW_TILE // ML_CHUNK):
        gt = gates[cc * ML_CHUNK:(cc + 1) * ML_CHUNK, :].T
        for h in range(ML_HEADS):
            gt_ref[h, cc] = gt[h * GATE_ROWS:(h + 1) * GATE_ROWS, :]


def _inproj(x2, g1, w_pad, wq_t, conv_w):
    t = x2.shape[0]
    halo_blocks = ROW_TILE // HALO
    chunks = ROW_TILE // ML_CHUNK
    return pl.pallas_call(
        _inproj_kernel,
        grid=(t // ROW_TILE,),
        in_specs=[
            pl.BlockSpec((ROW_TILE, D_MODEL), lambda i: (i, 0)),
            pl.BlockSpec((HALO, D_MODEL), lambda i: (jnp.maximum(i * halo_blocks - 1, 0), 0)),
            pl.BlockSpec((HALO, D_MODEL), lambda i: (jnp.minimum((i + 1) * halo_blocks, t // HALO - 1), 0)),
            pl.BlockSpec((1, D_MODEL), lambda i: (0, 0)),
            pl.BlockSpec((D_MODEL, PROJ_PAD), lambda i: (0, 0), pipeline_mode=pl.Buffered(1)),
            pl.BlockSpec((ATT_WIDTH, D_MODEL), lambda i: (0, 0), pipeline_mode=pl.Buffered(1)),
            pl.BlockSpec((3, 2 * ML_WIDTH), lambda i: (0, 0)),
        ],
        out_specs=[
            pl.BlockSpec((ROW_TILE, PROJ_OUT), lambda i: (i, 0)),
            pl.BlockSpec((ROW_TILE // BLOCK, ATT_WIDTH, BLOCK), lambda i: (i, 0, 0)),
            pl.BlockSpec((ML_HEADS, chunks, GATE_ROWS, LANES), lambda i: (0, i, 0, 0)),
            pl.BlockSpec((ML_HEADS, chunks, ML_HEAD_DIM, ML_CHUNK), lambda i: (0, i, 0, 0)),
            pl.BlockSpec((ML_HEADS, chunks, AUG_ROWS, ML_CHUNK), lambda i: (0, i, 0, 0)),
        ],
        out_shape=[
            jax.ShapeDtypeStruct((t, PROJ_OUT), BF16),
            jax.ShapeDtypeStruct((t // BLOCK, ATT_WIDTH, BLOCK), BF16),
            jax.ShapeDtypeStruct((ML_HEADS, t // ML_CHUNK, GATE_ROWS, LANES), F32),
            jax.ShapeDtypeStruct((ML_HEADS, t // ML_CHUNK, ML_HEAD_DIM, ML_CHUNK), BF16),
            jax.ShapeDtypeStruct((ML_HEADS, t // ML_CHUNK, AUG_ROWS, ML_CHUNK), BF16),
        ],
        scratch_shapes=[
            pltpu.VMEM((ROW_TILE + 2 * HALO, ML_WIDTH), F32),
            pltpu.VMEM((ROW_TILE + 2 * HALO, ML_WIDTH), F32),
            pltpu.VMEM((ROW_TILE, ML_WIDTH), F32),
        ],
        compiler_params=pltpu.CompilerParams(
            dimension_semantics=("arbitrary",), vmem_limit_bytes=VMEM_LIMIT),
        name="inproj",
    )(x2, x2, x2, g1, w_pad, wq_t, conv_w)


def _bias_kernel(rb_ref, bucket_ref, out_ref):
    bucket = bucket_ref[...]
    c = lax.broadcasted_iota(jnp.int32, bucket.shape, 0)
    r = lax.broadcasted_iota(jnp.int32, bucket.shape, 1)
    valid = jnp.abs(c - BLOCK - r) <= WINDOW
    for h in range(ATT_HEADS):
        acc = jnp.zeros(bucket.shape, F32)
        for b in range(REL_BUCKETS):
            acc = jnp.where(bucket == b, rb_ref[b, h], acc)
        out_ref[h] = jnp.where(valid, acc * LOG2_E, -jnp.inf)


def _bias_table(rel_bias, bucket_t):
    return pl.pallas_call(
        _bias_kernel,
        in_specs=[
            pl.BlockSpec(memory_space=pltpu.SMEM),
            pl.BlockSpec(memory_space=pltpu.VMEM),
        ],
        out_specs=pl.BlockSpec(memory_space=pltpu.VMEM),
        out_shape=jax.ShapeDtypeStruct((ATT_HEADS, 3 * BLOCK, BLOCK), F32),
        name="bias_table",
    )(rel_bias, bucket_t)


N_QBLOCKS = SEQ // BLOCK
ATT_SLOT_HEADS = tuple(p // 2 + 4 * (p % 2) for p in range(ATT_HEADS))


BAND = 3 * BLOCK
ATT_AUG_ROWS = LANES + 16
LOG2_E = math.log2(math.e)


def _attn_kernel(sink_ref, qt_ref, k_ref, v_ref, bias_ref, out_ref,
                 kz_ref, vtz_ref, sa_ref, sb_ref, pa_ref, pb_ref, la_ref, lb_ref):
    half = ATT_HEAD_DIM
    kf = k_ref[...].astype(F32)
    lane_s = lax.broadcasted_iota(jnp.int32, kf.shape, 1)
    zero_blk = jnp.zeros((BLOCK, LANES), BF16)
    row_t = lax.broadcasted_iota(jnp.int32, (ATT_AUG_ROWS - LANES, BLOCK), 0)
    ones_rows = [jnp.where(row_t == kv, 1.0, 0.0).astype(BF16) for kv in range(2)]
    for kv in range(2):
        kz_ref[kv, 0:BLOCK, :] = zero_blk
        kz_ref[kv, SEQ + BLOCK:SEQ + 2 * BLOCK, :] = zero_blk
        vtz_ref[kv, 0] = jnp.concatenate([zero_blk, ones_rows[kv]], axis=0)
        vtz_ref[kv, N_QBLOCKS + 1] = jnp.concatenate([zero_blk, ones_rows[kv]], axis=0)
    kz_ref[0, BLOCK:SEQ + BLOCK, :] = jnp.where(lane_s < half, kf, 0.0).astype(BF16)
    kz_ref[1, BLOCK:SEQ + BLOCK, :] = jnp.where(lane_s < half, 0.0, kf).astype(BF16)
    row_b = lax.broadcasted_iota(jnp.int32, (LANES, BLOCK), 0)
    for jb in range(N_QBLOCKS):
        vt = v_ref[jb * BLOCK:(jb + 1) * BLOCK, :].astype(F32).T
        vtz_ref[0, jb + 1] = jnp.concatenate([jnp.where(row_b < half, vt, 0.0).astype(BF16), ones_rows[0]], axis=0)
        vtz_ref[1, jb + 1] = jnp.concatenate([jnp.where(row_b < half, 0.0, vt).astype(BF16), ones_rows[1]], axis=0)
    row_o = lax.broadcasted_iota(jnp.int32, (LANES, 2 * BLOCK), 0)
    neg_blk = jnp.full((BLOCK, BLOCK), -jnp.inf, F32)

    def start(j):
        return j * BLOCK if isinstance(j, int) else pl.multiple_of(j * BLOCK, BLOCK)

    def stage_scores(j, s_ref):
        qt = qt_ref[j]
        k0 = start(j)
        kzb = jnp.concatenate([kz_ref[0, pl.ds(k0, BAND), :], kz_ref[1, pl.ds(k0, BAND), :]], axis=0)
        for pair in range(2):
            rows = 2 * LANES * pair
            rhs = jnp.concatenate([qt[rows:rows + LANES, :], qt[rows + LANES:rows + 2 * LANES, :]], axis=1)
            s_all = _dot(kzb, rhs)
            for kv in range(2):
                for t in range(2):
                    rs = slice(kv * BAND, (kv + 1) * BAND)
                    cs = slice(t * BLOCK, (t + 1) * BLOCK)
                    s_ref[pair, rs, cs] = s_all[rs, cs] + bias_ref[2 * pair + t + 4 * kv]

    def stage_softmax(s_ref, p_ref, l_ref, edge):
        for pair in range(2):
            for kv in range(2):
                for t in range(2):
                    h = 2 * pair + t + 4 * kv
                    rs = slice(kv * BAND, (kv + 1) * BAND)
                    cs = slice(t * BLOCK, (t + 1) * BLOCK)
                    s = s_ref[pair, rs, cs]
                    if edge < 0:
                        s = jnp.concatenate([neg_blk, s[BLOCK:]], axis=0)
                    elif edge > 0:
                        s = jnp.concatenate([s[:2 * BLOCK], neg_blk], axis=0)
                    sink = sink_ref[h] * LOG2_E
                    m = jnp.maximum(jnp.max(s, axis=0, keepdims=True), sink)
                    l_ref[pair, kv:kv + 1, cs] = jnp.exp2(sink - m)
                    p_ref[pair, rs, cs] = jnp.exp2(s - m).astype(BF16)

    def stage_pv(j, p_ref, l_ref):
        q0 = start(j)
        vtb = jnp.concatenate([vtz_ref[kv, j + i] for kv in range(2) for i in range(3)], axis=1)
        for pair in range(2):
            ot = _dot(vtb, p_ref[pair])
            inv = 1.0 / (ot[LANES:LANES + 2, :] + l_ref[pair, 0:2, :])
            ot = ot[:LANES, :] * jnp.where(row_o < half, inv[0:1, :], inv[1:2, :])
            for t in range(2):
                c0 = (2 * pair + t) * LANES
                out_ref[pl.ds(q0, BLOCK), c0:c0 + LANES] = ot[:, t * BLOCK:(t + 1) * BLOCK].T.astype(BF16)

    stage_scores(0, sa_ref)
    stage_scores(1, sb_ref)
    stage_softmax(sa_ref, pa_ref, la_ref, -1)

    def two_blocks(ii, carry):
        j = 2 * ii
        stage_pv(j - 2, pa_ref, la_ref)
        stage_scores(j, sa_ref)
        stage_softmax(sb_ref, pb_ref, lb_ref, 0)
        stage_pv(j - 1, pb_ref, lb_ref)
        stage_scores(j + 1, sb_ref)
        stage_softmax(sa_ref, pa_ref, la_ref, 0)
        return carry

    lax.fori_loop(1, N_QBLOCKS // 2, two_blocks, 0)
    stage_softmax(sb_ref, pb_ref, lb_ref, 1)
    stage_pv(N_QBLOCKS - 2, pa_ref, la_ref)
    stage_pv(N_QBLOCKS - 1, pb_ref, lb_ref)


def _attention(qt, proj, bias, sink, batch):
    t = proj.shape[0]
    kv_spec = lambda off: pl.BlockSpec((SEQ, ATT_KV_WIDTH), lambda b: (b, off // ATT_KV_WIDTH))
    return pl.pallas_call(
        _attn_kernel,
        grid=(batch,),
        in_specs=[
            pl.BlockSpec(memory_space=pltpu.SMEM),
            pl.BlockSpec((N_QBLOCKS, ATT_WIDTH, BLOCK), lambda b: (b, 0, 0)),
            kv_spec(OFF_KA),
            kv_spec(OFF_VA),
            pl.BlockSpec((ATT_HEADS, 3 * BLOCK, BLOCK), lambda b: (0, 0, 0)),
        ],
        out_specs=pl.BlockSpec((SEQ, ATT_WIDTH), lambda b: (b, 0)),
        out_shape=jax.ShapeDtypeStruct((t, ATT_WIDTH), BF16),
        scratch_shapes=[
            pltpu.VMEM((2, SEQ + 2 * BLOCK, ATT_KV_WIDTH), BF16),
            pltpu.VMEM((2, N_QBLOCKS + 2, ATT_AUG_ROWS, BLOCK), BF16),
            pltpu.VMEM((2, 2 * BAND, 2 * BLOCK), F32),
            pltpu.VMEM((2, 2 * BAND, 2 * BLOCK), F32),
            pltpu.VMEM((2, 2 * BAND, 2 * BLOCK), BF16),
            pltpu.VMEM((2, 2 * BAND, 2 * BLOCK), BF16),
            pltpu.VMEM((2, 8, 2 * BLOCK), F32),
            pltpu.VMEM((2, 8, 2 * BLOCK), F32),
        ],
        compiler_params=pltpu.CompilerParams(
            dimension_semantics=("arbitrary",), vmem_limit_bytes=VMEM_LIMIT),
        name="attention",
    )(sink, qt, proj, proj, bias)


def _log_sigmoid(x):
    return jnp.minimum(x, 0.0) - jnp.log1p(jnp.exp(-jnp.abs(x)))


def _split3(x):
    hi = x.astype(BF16)
    r1 = x - hi.astype(F32)
    mid = r1.astype(BF16)
    lo = (r1 - mid.astype(F32)).astype(BF16)
    return hi, mid, lo


ROWS_PER_CHAIN = 6
N_CHAIN_ROWS = 2 * ML_HEADS * ROWS_PER_CHAIN


def _mlstm_kernel(qt_ref, ks_ref, vt_ref, gt_ref, gb_ref, out_ref, cf_ref, cb_ref, rows_ref, rhs_ref):
    L = ML_CHUNK
    d = ML_HEAD_DIM
    nh = ML_HEADS
    si = lax.broadcasted_iota(jnp.int32, (L, L), 0)
    ti = lax.broadcasted_iota(jnp.int32, (L, L), 1)
    masks = (si <= ti, si >= ti)
    ones_le = jnp.where(si <= ti, 1.0, 0.0).astype(BF16)
    ones_ge = jnp.where(si >= ti, 1.0, 0.0).astype(BF16)
    eye = jnp.where(si == ti, 1.0, 0.0).astype(BF16)

    n_rows = nh * N_CHUNKS
    gates = (gt_ref[...] + gb_ref[...][:, None]).reshape(n_rows * GATE_ROWS, L)
    gate_parts = _split3(gates)
    sel_r = lax.broadcasted_iota(jnp.int32, (n_rows, n_rows * GATE_ROWS), 0)
    sel_c = lax.broadcasted_iota(jnp.int32, (n_rows, n_rows * GATE_ROWS), 1)
    lane2 = lax.broadcasted_iota(jnp.int32, (n_rows, L), 1)
    chunk2 = lax.broadcasted_iota(jnp.int32, (n_rows, L), 0) % N_CHUNKS

    def gate_rows(kind):
        pick = jnp.where(sel_c == sel_r * GATE_ROWS + kind, 1.0, 0.0).astype(BF16)
        return sum(_dot(pick, p) for p in gate_parts)

    dirs = (0, 1)
    cum_ones = (ones_le, ones_ge)
    li = [gate_rows(2 * dr) for dr in dirs]
    lf = [_log_sigmoid(gate_rows(2 * dr + 1)) for dr in dirs]
    b = [sum(_dot(p, cum_ones[dr]) for p in _split3(lf[dr])) for dr in dirs]
    b_last = [jnp.broadcast_to(b[0][:, L - 1:L], b[0].shape), jnp.broadcast_to(b[1][:, 0:1], b[1].shape)]
    a = [b_last[dr] - b[dr] + li[dr] for dr in dirs]
    a_max = [jnp.broadcast_to(jnp.max(a[dr], axis=-1, keepdims=True), a[dr].shape) for dr in dirs]
    r = [li[dr] - b[dr] for dr in dirs]
    cm = list(r)
    for sh in (1, 2, 4, 8, 16, 32, 64):
        cm = [jnp.maximum(cm[0], jnp.where(lane2 >= sh, pltpu.roll(cm[0], sh, 1), -jnp.inf)),
              jnp.maximum(cm[1], jnp.where(lane2 < L - sh, pltpu.roll(cm[1], L - sh, 1), -jnp.inf))]
    sa, sb = list(b_last), list(a_max)
    for sh in (1, 2, 4, 8):
        ok = (chunk2 >= sh, chunk2 < N_CHUNKS - sh)
        shift = (sh, n_rows - sh)
        pa = [pltpu.roll(sa[dr], shift[dr], 0) for dr in dirs]
        pb = [pltpu.roll(sb[dr], shift[dr], 0) for dr in dirs]
        sb = [jnp.where(ok[dr], jnp.maximum(pb[dr] + sa[dr], sb[dr]), sb[dr]) for dr in dirs]
        sa = [jnp.where(ok[dr], pa[dr] + sa[dr], sa[dr]) for dr in dirs]
    m_new = [jnp.maximum(sa[dr], sb[dr]) for dr in dirs]
    m_prev = [jnp.where(chunk2 >= 1, pltpu.roll(m_new[0], 1, 0), 0.0),
              jnp.where(chunk2 < N_CHUNKS - 1, pltpu.roll(m_new[1], n_rows - 1, 0), 0.0)]
    m_t = [b[dr] + jnp.maximum(m_prev[dr], cm[dr]) for dr in dirs]
    k_scale = ML_HEAD_DIM ** -0.5
    for dr in dirs:
        for j, val in enumerate((b[dr] - m_t[dr] + math.log(k_scale), jnp.exp(b[dr] + m_prev[dr] - m_t[dr]),
                                 jnp.exp(-m_t[dr]), k_scale * jnp.exp(a[dr] - m_new[dr]), r[dr],
                                 jnp.exp(b_last[dr] + m_prev[dr] - m_new[dr]))):
            for hh in range(nh):
                rows_ref[:, (2 * hh + dr) * ROWS_PER_CHAIN + j, :] = val[hh * N_CHUNKS:(hh + 1) * N_CHUNKS]

    cf_ref[...] = jnp.zeros(cf_ref.shape, F32)
    cb_ref[...] = jnp.zeros(cb_ref.shape, F32)

    chains = [(hh, direction) for hh in range(nh) for direction in range(2)]
    c_refs = (cf_ref, cb_ref)

    def chain_rows(i):
        chunks = (i, N_CHUNKS - 1 - i)
        tiles = (rows_ref[chunks[0]], rows_ref[chunks[1]])

        def row(n, j):
            hh, direction = chains[n]
            base = (2 * hh + direction) * ROWS_PER_CHAIN + j
            return tiles[direction][base:base + 1, :]

        return chunks, row

    def k_chunk(hh, c):
        return ks_ref[pl.ds(pl.multiple_of(c * L, L), L), hh * d:(hh + 1) * d]

    def intra_matmuls(i):
        chunks, row = chain_rows(i)
        r_rows = jnp.concatenate([row(n, 4) for n in range(len(chains))], axis=0)
        r_cols = sum(_dot_nt(eye, p) for p in _split3(r_rows))
        qtcs = [qt_ref[hh, chunks[direction]] for hh, direction in chains]
        kqs = [_dot(k_chunk(hh, chunks[direction]), qtcs[n]) for n, (hh, direction) in enumerate(chains)]
        return row, r_cols, qtcs, kqs

    def intra_store(ctx):
        row, r_cols, qtcs, kqs = ctx
        for n, (hh, direction) in enumerate(chains):
            st = kqs[n] * jnp.exp(jnp.where(masks[direction], r_cols[:, n:n + 1] + row(n, 0), -jnp.inf))
            wq = qtcs[n].astype(F32) * row(n, 1)
            rhs_ref[n] = jnp.concatenate([st.astype(BF16), wq.astype(BF16)], axis=0)

    def state_matmuls(i):
        chunks, row = chain_rows(i)
        vtas = [vt_ref[hh, chunks[direction]] for hh, direction in chains]
        states = [c_refs[direction][hh] for hh, direction in chains]
        nds = [_dot(jnp.concatenate([vtas[n], states[n].astype(BF16)], axis=1), rhs_ref[n])
               for n in range(len(chains))]
        return chunks, row, vtas, states, nds

    def state_finish(ctx, accumulate):
        chunks, row, vtas, states, nds = ctx
        for n, (hh, direction) in enumerate(chains):
            nd = nds[n]
            h = nd[:d, :] * (1.0 / jnp.maximum(jnp.abs(nd[d:d + 1, :]), row(n, 2)))
            if accumulate:
                out_ref[hh, chunks[direction]] += h
            else:
                out_ref[hh, chunks[direction]] = h
        for n, (hh, direction) in enumerate(chains):
            vw = (vtas[n].astype(F32) * row(n, 3)).astype(BF16)
            c_refs[direction][hh] = row(n, 5) * states[n] + _dot(vw, k_chunk(hh, chunks[direction]))

    def body(i, carry, accumulate):
        ahead = intra_matmuls(i + 1)
        cur = state_matmuls(i)
        intra_store(ahead)
        state_finish(cur, accumulate)
        return carry

    intra_store(intra_matmuls(0))
    lax.fori_loop(0, N_CHUNKS // 2, functools.partial(body, accumulate=False), 0)
    lax.fori_loop(N_CHUNKS // 2, N_CHUNKS - 1, functools.partial(body, accumulate=True), 0)
    state_finish(state_matmuls(N_CHUNKS - 1), True)


def _mlstm(qmt, proj, vta, gt, gbias, batch):
    t = proj.shape[0]
    per_chunk = lambda rows: pl.BlockSpec((ML_HEADS, N_CHUNKS, rows, ML_CHUNK), lambda b: (0, b, 0, 0))
    return pl.pallas_call(
        _mlstm_kernel,
        grid=(batch,),
        in_specs=[
            per_chunk(ML_HEAD_DIM),
            pl.BlockSpec((SEQ, ML_WIDTH), lambda b: (b, 0)),
            per_chunk(AUG_ROWS),
            pl.BlockSpec((ML_HEADS, N_CHUNKS, GATE_ROWS, LANES), lambda b: (0, b, 0, 0)),
            pl.BlockSpec((ML_HEADS, GATE_ROWS, 1), lambda b: (0, 0, 0)),
        ],
        out_specs=per_chunk(ML_HEAD_DIM),
        out_shape=jax.ShapeDtypeStruct((ML_HEADS, t // ML_CHUNK, ML_HEAD_DIM, ML_CHUNK), F32),
        scratch_shapes=[
            pltpu.VMEM((ML_HEADS, AUG_ROWS, ML_HEAD_DIM), F32),
            pltpu.VMEM((ML_HEADS, AUG_ROWS, ML_HEAD_DIM), F32),
            pltpu.VMEM((N_CHUNKS, N_CHAIN_ROWS, LANES), F32),
            pltpu.VMEM((2 * ML_HEADS, 2 * ML_CHUNK, ML_CHUNK), BF16),
        ],
        compiler_params=pltpu.CompilerParams(
            dimension_semantics=("arbitrary",), vmem_limit_bytes=VMEM_LIMIT),
        name="mlstm",
    )(qmt, proj, vta, gt, gbias)


def _ffn_kernel(att_ref, ht_ref, o_ref, ng_ref, x_ref, wo_ref, g2_ref, wup_ref, wdn_ref, gf_ref, out_ref,
                ml_ref):
    for hh in range(ML_HEADS):
        cols = slice(hh * ML_HEAD_DIM, (hh + 1) * ML_HEAD_DIM)
        for cc in range(ROW_TILE // ML_CHUNK):
            rows = slice(cc * ML_CHUNK, (cc + 1) * ML_CHUNK)
            h = _rms(ht_ref[hh, cc].T, ng_ref[:, cols])
            gate = o_ref[rows, cols].astype(F32)
            ml_ref[rows, cols] = (h * (1.0 / (1.0 + jnp.exp(-gate)))).astype(BF16)
    x1 = (x_ref[...] + _dot(att_ref[...], wo_ref[:ATT_WIDTH, :])
          + _dot(ml_ref[...], wo_ref[ATT_WIDTH:, :]))
    hn = _rms(x1, g2_ref[...]).astype(BF16)
    out_ref[...] = x1
    for c0 in range(0, D_FF, FF_CHUNK):
        hid = _dot(hn, wup_ref[:, c0:c0 + FF_CHUNK])
        act = jnp.square(jnp.maximum(hid, 0.0)).astype(BF16)
        out_ref[...] += _dot(act, wdn_ref[c0:c0 + FF_CHUNK, :])
    out_ref[...] = _rms(out_ref[...], gf_ref[...])


def _ffn(att, ht, proj, ng, x2, w_out, g2, w_up, w_down, gf):
    t = x2.shape[0]
    row = lambda w: pl.BlockSpec((ROW_TILE, w), lambda i: (i, 0))
    whole = lambda a: pl.BlockSpec(a.shape, lambda i: (0, 0), pipeline_mode=pl.Buffered(1))
    chunks = ROW_TILE // ML_CHUNK
    return pl.pallas_call(
        _ffn_kernel,
        grid=(t // ROW_TILE,),
        in_specs=[row(ATT_WIDTH),
                  pl.BlockSpec((ML_HEADS, chunks, ML_HEAD_DIM, ML_CHUNK), lambda i: (0, i, 0, 0)),
                  pl.BlockSpec((ROW_TILE, ML_WIDTH), lambda i: (i, OFF_OM // ML_WIDTH)),
                  whole(ng), row(D_MODEL), whole(w_out), whole(g2), whole(w_up), whole(w_down), whole(gf)],
        out_specs=row(D_MODEL),
        out_shape=jax.ShapeDtypeStruct((t, D_MODEL), F32),
        scratch_shapes=[pltpu.VMEM((ROW_TILE, ML_WIDTH), BF16)],
        compiler_params=pltpu.CompilerParams(
            dimension_semantics=("arbitrary",), vmem_limit_bytes=VMEM_LIMIT),
        name="outproj_ffn",
    )(att, ht, proj, ng, x2, w_out, g2, w_up, w_down, gf)


def _t5_bucket(rel):
    nb = REL_BUCKETS // 2
    max_exact = nb // 2
    ret = jnp.where(rel > 0, nb, 0)
    n = jnp.abs(rel)
    nf = jnp.maximum(n, 1).astype(jnp.float32)
    large = max_exact + (jnp.log(nf / max_exact) / math.log(REL_MAX_DIST / max_exact)
                         * (nb - max_exact)).astype(jnp.int32)
    large = jnp.minimum(large, nb - 1)
    return ret + jnp.where(n < max_exact, n, large)


def kernel(x, norm1_g, w_in, b_gates, conv_w, ml_norm_g, sink_logits, w_out, norm2_g, w_up, w_down,
           rel_bias, final_g):
    batch, seq, d_model = x.shape
    assert (seq, d_model) == (SEQ, D_MODEL) and w_in.shape[0] == 1
    x2 = x.reshape(batch * seq, d_model)

    w = w_in[0]
    att_cols = ATT_WIDTH + 2 * ATT_KV_WIDTH
    gate_cols = w[:, att_cols + 4 * ML_WIDTH:].reshape(d_model, 4, ML_HEADS).transpose(0, 2, 1)
    gate_cols = jnp.pad(gate_cols, ((0, 0), (0, 0), (0, GATE_ROWS - 4))).reshape(d_model, -1)
    gate_cols = jnp.pad(gate_cols, ((0, 0), (0, LANES - gate_cols.shape[1])))
    w_pad = jnp.concatenate([w[:, att_cols:att_cols + 4 * ML_WIDTH], w[:, ATT_WIDTH:att_cols], gate_cols],
                            axis=1).astype(BF16)
    slots = jnp.asarray(ATT_SLOT_HEADS)
    wq_t = w[:, :ATT_WIDTH].reshape(d_model, ATT_HEADS, ATT_HEAD_DIM)[:, slots, :]
    wq_t = wq_t.reshape(d_model, ATT_WIDTH).T.astype(BF16)
    gbias = jnp.pad(b_gates[0].reshape(4, ML_HEADS).T, ((0, 0), (0, GATE_ROWS - 4)))[..., None]

    proj, qt, gt, qmt, vta = _inproj(x2, norm1_g, w_pad, wq_t, conv_w[0])

    c = jnp.arange(3 * BLOCK)[:, None]
    r = jnp.arange(BLOCK)[None, :]
    bucket_t = _t5_bucket(c - BLOCK - r).astype(jnp.int32)
    bias = _bias_table(rel_bias.astype(F32), bucket_t)
    att = _attention(qt, proj, bias, sink_logits[0].astype(F32), batch)

    ht = _mlstm(qmt, proj, vta, gt, gbias.astype(F32), batch)

    wo = w_out[0]
    wo_att = wo[:ATT_WIDTH].reshape(ATT_HEADS, ATT_HEAD_DIM, d_model)[slots].reshape(ATT_WIDTH, d_model)
    wo = jnp.concatenate([wo_att, wo[ATT_WIDTH:]], axis=0)
    out = _ffn(att, ht, proj, ml_norm_g, x2, wo.astype(BF16), norm2_g, w_up[0].astype(BF16),
               w_down[0].astype(BF16), final_g.reshape(1, d_model))
    return out.reshape(batch, seq, d_model)
```
